```python
import math
import jax, jax.numpy as jnp
from jax import lax
import numpy as np

D_MODEL = 2048
BATCH = 4
SEQ = 8192
DEPTH = 1

CHUNK = 64
Q_BLOCK = 128
SSM_WIDTH = D_MODEL // 2
SSM_GROUP = 16
SSM_GROUPS = SSM_WIDTH // SSM_GROUP
SSM_STATE = 64
ATTN_WIDTH = D_MODEL - SSM_WIDTH
ATTN_HEADS = 8
ATTN_V_DIM = ATTN_WIDTH // ATTN_HEADS
ATTN_QK_DIM = ATTN_V_DIM // 2
IN_WIDTH = SSM_WIDTH + 3 * ATTN_WIDTH
MLP_HIDDEN = 4 * D_MODEL
RMS_EPS = 1e-6
DT_MIN = 0.001
DT_MAX = 0.1

kernel_name = "hybrid_s5_diffattn_adaln_block"


def _rms_f32(x, g):
    xf = x.astype(jnp.float32)
    return xf * lax.rsqrt(jnp.mean(xf * xf, axis=-1, keepdims=True) + RMS_EPS) * g.astype(jnp.float32)


def _rmsnorm(x, g):
    return _rms_f32(x, g).astype(x.dtype)


def _s5_combine(e1, e2):
    a1r, a1i, b1r, b1i = e1
    a2r, a2i, b2r, b2i = e2
    return (a2r * a1r - a2i * a1i,
            a2r * a1i + a2i * a1r,
            a2r * b1r - a2i * b1i + b2r,
            a2r * b1i + a2i * b1r + b2i)


def _s5_mixer(u, lam_re, lam_im, b_re, b_im, c_re, c_im, d, log_step, w_glu, b_glu):
    f32 = jnp.float32
    bsz, seq, _ = u.shape
    n_chunks = seq // CHUNK
    u4 = u.astype(f32).reshape(bsz, seq, SSM_GROUPS, SSM_GROUP)
    lr = lam_re.astype(f32)
    li = lam_im.astype(f32)
    dt = jnp.exp(log_step.astype(f32))[:, None]
    mag = jnp.exp(dt * lr)
    ar = mag * jnp.cos(dt * li)
    ai = mag * jnp.sin(dt * li)
    den = lr * lr + li * li
    zr = ar - 1.0
    kr = (zr * lr + ai * li) / den
    ki = (ai * lr - zr * li) / den
    br = b_re.astype(f32)
    bi = b_im.astype(f32)
    bbar_r = kr[..., None] * br - ki[..., None] * bi
    bbar_i = kr[..., None] * bi + ki[..., None] * br
    steps = jnp.arange(1, CHUNK + 1, dtype=f32)[:, None, None]
    pmag = jnp.exp(steps * dt * lr)
    pw_r = pmag * jnp.cos(steps * dt * li)
    pw_i = pmag * jnp.sin(steps * dt * li)
    a_r = jnp.broadcast_to(ar, (bsz, CHUNK, SSM_GROUPS, SSM_STATE))
    a_i = jnp.broadcast_to(ai, (bsz, CHUNK, SSM_GROUPS, SSM_STATE))
    cr = c_re.astype(f32)
    ci = c_im.astype(f32)
    u_chunks = u4.reshape(bsz, n_chunks, CHUNK, SSM_GROUPS, SSM_GROUP).transpose(1, 0, 2, 3, 4)

    def step(carry, uc):
        hr0, hi0 = carry
        bu_r = jnp.einsum('bsgh,gph->bsgp', uc, bbar_r)
        bu_i = jnp.einsum('bsgh,gph->bsgp', uc, bbar_i)
        _, _, loc_r, loc_i = lax.associative_scan(_s5_combine, (a_r, a_i, bu_r, bu_i), axis=1)
        hr = loc_r + pw_r * hr0[:, None] - pw_i * hi0[:, None]
        hi = loc_i + pw_r * hi0[:, None] + pw_i * hr0[:, None]
        y = jnp.einsum('gqp,bsgp->bsgq', cr, hr) - jnp.einsum('gqp,bsgp->bsgq', ci, hi)
        return (hr[:, -1], hi[:, -1]), y

    init = (jnp.zeros((bsz, SSM_GROUPS, SSM_STATE), f32), jnp.zeros((bsz, SSM_GROUPS, SSM_STATE), f32))
    _, ys = lax.scan(step, init, u_chunks)
    y = ys.transpose(1, 0, 2, 3, 4).reshape(bsz, seq, SSM_GROUPS, SSM_GROUP) + d.astype(f32) * u4
    y = jax.nn.gelu(y.reshape(bsz, seq, SSM_WIDTH), approximate=False).astype(u.dtype)
    return y * jax.nn.sigmoid(y @ w_glu + b_glu)


def _diff_attention(q, k, v, g_q, g_k, lq1, lk1, lq2, lk2, g_subln, lambda_init):
    f32 = jnp.float32
    out_dtype = v.dtype
    bsz, seq, _ = q.shape
    q = _rms_f32(q.reshape(bsz, seq, ATTN_HEADS, 2, ATTN_QK_DIM), g_q) * (ATTN_QK_DIM ** -0.5)
    k = _rms_f32(k.reshape(bsz, seq, ATTN_HEADS, 2, ATTN_QK_DIM), g_k)
    v = v.reshape(bsz, seq, ATTN_HEADS, ATTN_V_DIM).astype(f32)
    lam = (jnp.exp(jnp.sum(lq1.astype(f32) * lk1.astype(f32)))
           - jnp.exp(jnp.sum(lq2.astype(f32) * lk2.astype(f32))) + lambda_init)
    n_blocks = seq // Q_BLOCK
    qb = q.reshape(bsz, n_blocks, Q_BLOCK, ATTN_HEADS, 2, ATTN_QK_DIM).transpose(1, 0, 2, 3, 4, 5)
    k_chunk = jnp.arange(seq) // CHUNK

    def block(args):
        qblk, idx = args
        s = jnp.einsum('bqhcd,bkhcd->bhcqk', qblk, k)
        q_chunk = (idx * Q_BLOCK + jnp.arange(Q_BLOCK)) // CHUNK
        mask = k_chunk[None, :] <= q_chunk[:, None]
        p = jax.nn.softmax(jnp.where(mask, s, -jnp.inf), axis=-1)
        w = p[:, :, 0] - lam * p[:, :, 1]
        return jnp.einsum('bhqk,bkhe->bqhe', w, v)

    o = lax.map(block, (qb, jnp.arange(n_blocks)))
    o = o.transpose(1, 0, 2, 3, 4).reshape(bsz, seq, ATTN_HEADS, ATTN_V_DIM)
    o = _rms_f32(o, g_subln) * (1.0 - lambda_init)
    return o.reshape(bsz, seq, ATTN_WIDTH).astype(out_dtype)


def setup_inputs(seed: int = 0) -> dict:
    key = jax.random.key(seed)
    ks = jax.random.split(key, 32)
    f32 = jnp.float32

    def nrm(k, shape, scale):
        return jax.random.normal(k, shape, f32) * scale

    G, P, H = SSM_GROUPS, SSM_STATE, SSM_GROUP
    n_idx = jnp.arange(P, dtype=f32)
    return {
        "x": nrm(ks[0], (BATCH, SEQ, D_MODEL), 1.0),
        "c": nrm(ks[1], (BATCH, D_MODEL), 1.0),
        "w_ada": nrm(ks[2], (DEPTH, D_MODEL, 6 * D_MODEL), 0.5 * D_MODEL ** -0.5),
        "b_ada": nrm(ks[3], (DEPTH, 6 * D_MODEL), 0.02),
        "g_norm_mix": 1.0 + nrm(ks[4], (DEPTH, D_MODEL), 0.02),
        "g_norm_mlp": 1.0 + nrm(ks[5], (DEPTH, D_MODEL), 0.02),
        "w_in": nrm(ks[6], (DEPTH, D_MODEL, IN_WIDTH), D_MODEL ** -0.5),
        "ssm_lambda_re": -0.5 * jnp.exp(nrm(ks[7], (DEPTH, G, P), 0.05)),
        "ssm_lambda_im": jnp.pi * n_idx + nrm(ks[8], (DEPTH, G, P), 0.01),
        "ssm_b_re": nrm(ks[9], (DEPTH, G, P, H), (2 * H) ** -0.5),
        "ssm_b_im": nrm(ks[10], (DEPTH, G, P, H), (2 * H) ** -0.5),
        "ssm_c_re": nrm(ks[11], (DEPTH, G, H, P), P ** -0.5),
        "ssm_c_im": nrm(ks[12], (DEPTH, G, H, P), P ** -0.5),
        "ssm_d": nrm(ks[13], (DEPTH, G, H), 1.0),
        "ssm_log_step": jax.random.uniform(ks[14], (DEPTH, G), f32, math.log(DT_MIN), math.log(DT_MAX)),
        "w_glu": nrm(ks[15], (DEPTH, SSM_WIDTH, SSM_WIDTH), SSM_WIDTH ** -0.5),
        "b_glu": nrm(ks[16], (DEPTH, SSM_WIDTH), 0.02),
        "g_q": 1.0 + nrm(ks[17], (DEPTH, ATTN_QK_DIM), 0.02),
        "g_k": 1.0 + nrm(ks[18], (DEPTH, ATTN_QK_DIM), 0.02),
        "lambda_q1": nrm(ks[19], (DEPTH, ATTN_QK_DIM), 0.1),
        "lambda_k1": nrm(ks[20], (DEPTH, ATTN_QK_DIM), 0.1),
        "lambda_q2": nrm(ks[21], (DEPTH, ATTN_QK_DIM), 0.1),
        "lambda_k2": nrm(ks[22], (DEPTH, ATTN_QK_DIM), 0.1),
        "g_subln": 1.0 + nrm(ks[23], (DEPTH, ATTN_V_DIM), 0.02),
        "w_out": nrm(ks[24], (DEPTH, D_MODEL, D_MODEL), D_MODEL ** -0.5),
        "w_mlp1": nrm(ks[25], (DEPTH, D_MODEL, MLP_HIDDEN), D_MODEL ** -0.5),
        "w_mlp2": nrm(ks[26], (DEPTH, MLP_HIDDEN, D_MODEL), MLP_HIDDEN ** -0.5),
    }


def reference(x, c, w_ada, b_ada, g_norm_mix, g_norm_mlp, w_in, ssm_lambda_re, ssm_lambda_im,
              ssm_b_re, ssm_b_im, ssm_c_re, ssm_c_im, ssm_d, ssm_log_step, w_glu, b_glu,
              g_q, g_k, lambda_q1, lambda_k1, lambda_q2, lambda_k2, g_subln, w_out, w_mlp1, w_mlp2):
    c_act = jax.nn.silu(c)
    for l in range(DEPTH):
        lambda_init = 0.8 - 0.6 * math.exp(-0.3 * l)
        mod = c_act @ w_ada[l] + b_ada[l]
        shift1, scale1, gate1, shift2, scale2, gate2 = jnp.split(mod, 6, axis=-1)
        h = _rmsnorm(x, g_norm_mix[l]) * (1.0 + scale1[:, None]) + shift1[:, None]
        proj = h @ w_in[l]
        u, q, k, v = jnp.split(proj, [SSM_WIDTH, SSM_WIDTH + ATTN_WIDTH, SSM_WIDTH + 2 * ATTN_WIDTH], axis=-1)
        y_ssm = _s5_mixer(u, ssm_lambda_re[l], ssm_lambda_im[l], ssm_b_re[l], ssm_b_im[l],
                          ssm_c_re[l], ssm_c_im[l], ssm_d[l], ssm_log_step[l], w_glu[l], b_glu[l])
        y_att = _diff_attention(q, k, v, g_q[l], g_k[l], lambda_q1[l], lambda_k1[l],
                                lambda_q2[l], lambda_k2[l], g_subln[l], lambda_init)
        mixed = jnp.concatenate([y_ssm, y_att], axis=-1) @ w_out[l]
        x = x + gate1[:, None] * mixed
        h = _rmsnorm(x, g_norm_mlp[l]) * (1.0 + scale2[:, None]) + shift2[:, None]
        x = x + gate2[:, None] * (jnp.square(jax.nn.relu(h @ w_mlp1[l])) @ w_mlp2[l])
    return x
```

```python
import functools
import math

import jax
import jax.numpy as jnp
from jax import lax
from jax.experimental import pallas as pl
from jax.experimental.pallas import tpu as pltpu

CHUNK = 64
RMS_EPS = 1e-6
LANES = 128
MXU_DIM = 256
VMEM_LIMIT = 56 * 1024 * 1024

F32 = jnp.float32
BF16 = jnp.bfloat16


def _params(*sem):
    return pltpu.CompilerParams(dimension_semantics=sem, vmem_limit_bytes=VMEM_LIMIT)


def _const_spec(shape):
    nd = len(shape)
    return pl.BlockSpec(shape, lambda *_: (0,) * nd, pipeline_mode=pl.Buffered(1))


def _ada_body(c_ref, w_ref, b_ref, o_ref):
    c = c_ref[...]
    c_act = c * jax.nn.sigmoid(c)
    o_ref[...] = jnp.dot(c_act, w_ref[...], preferred_element_type=F32,
                         precision=lax.Precision.HIGHEST) + b_ref[...]


def _ada_call(c, w_ada, b_ada, tn=1024):
    bsz, d = c.shape
    n = w_ada.shape[1]
    return pl.pallas_call(
        _ada_body,
        grid=(n // tn,),
        in_specs=[pl.BlockSpec((bsz, d), lambda j: (0, 0)),
                  pl.BlockSpec((d, tn), lambda j: (0, j)),
                  pl.BlockSpec((1, tn), lambda j: (0, j))],
        out_specs=pl.BlockSpec((bsz, tn), lambda j: (0, j)),
        out_shape=jax.ShapeDtypeStruct((bsz, n), F32),
        compiler_params=_params("parallel"),
        name="ada",
    )(c, w_ada, b_ada.reshape(1, n))


def _modulated_norm(x, g, scale, shift):
    ms = jnp.mean(x * x, axis=-1, keepdims=True)
    return x * lax.rsqrt(ms + RMS_EPS) * g * (1.0 + scale) + shift


def _inproj_body(x_ref, g_ref, sc_ref, sh_ref, wuk_ref, wqv_ref, seg_ref, gk_ref, gq_ref,
                 u_ref, k_ref, q_ref, v_ref, *, width, qk_dim, tq):
    x = x_ref[0]
    h = _modulated_norm(x, g_ref[...], sc_ref[0], sh_ref[0]).astype(BF16)
    uk = jnp.dot(h, wuk_ref[...], preferred_element_type=F32)
    u_ref[0] = uk[:, :width].astype(BF16)
    k = uk[:, width:]
    parts = []
    for t in range(width // MXU_DIM):
        kt = k[:, t * MXU_DIM:(t + 1) * MXU_DIM]
        ssq = jnp.dot((kt * kt).astype(BF16), seg_ref[...], preferred_element_type=F32)
        parts.append(kt * lax.rsqrt(ssq * (1.0 / qk_dim) + RMS_EPS))
    k_ref[0] = (jnp.concatenate(parts, axis=1) * gk_ref[...]).astype(BF16)
    qv = lax.dot_general(wqv_ref[...], h, (((1,), (1,)), ((), ())), preferred_element_type=F32)
    tm = qv.shape[1]
    q3 = qv[:width].reshape(width // qk_dim, qk_dim, tm)
    ssq = jnp.sum(q3 * q3, axis=1, keepdims=True)
    qn = (q3 * lax.rsqrt(ssq * (1.0 / qk_dim) + RMS_EPS)).reshape(width, tm) * gq_ref[...]
    vt = qv[width:]
    for r in range(tm // tq):
        q_ref[0, r] = qn[:, r * tq:(r + 1) * tq].astype(BF16)
        v_ref[0, r] = vt[:, r * tq:(r + 1) * tq].astype(BF16)


def _inproj_call(x, g, scale, shift, w_uk, w_qvt, seg, gk_row, gq_col, *, qk_dim, tm, tq):
    bsz, seq, d = x.shape
    width = w_uk.shape[1] // 2
    nq = seq // tq
    r = tm // tq
    tok = lambda b, i: (b, i, 0)
    per_b = lambda b, i: (b, 0, 0)
    body = functools.partial(_inproj_body, width=width, qk_dim=qk_dim, tq=tq)
    return pl.pallas_call(
        body,
        grid=(bsz, seq // tm),
        in_specs=[pl.BlockSpec((1, tm, d), tok),
                  _const_spec((1, d)),
                  pl.BlockSpec((1, 1, d), per_b),
                  pl.BlockSpec((1, 1, d), per_b),
                  _const_spec(w_uk.shape),
                  _const_spec(w_qvt.shape),
                  _const_spec(seg.shape),
                  _const_spec(gk_row.shape),
                  _const_spec(gq_col.shape)],
        out_specs=[pl.BlockSpec((1, tm, width), tok),
                   pl.BlockSpec((1, tm, width), tok),
                   pl.BlockSpec((1, r, width, tq), lambda b, i: (b, i, 0, 0)),
                   pl.BlockSpec((1, r, width, tq), lambda b, i: (b, i, 0, 0))],
        out_shape=[jax.ShapeDtypeStruct((bsz, seq, width), BF16),
                   jax.ShapeDtypeStruct((bsz, seq, width), BF16),
                   jax.ShapeDtypeStruct((bsz, nq, width, tq), BF16),
                   jax.ShapeDtypeStruct((bsz, nq, width, tq), BF16)],
        compiler_params=_params("parallel", "parallel"),
        name="inproj",
    )(x, g, scale, shift, w_uk, w_qvt, seg, gk_row, gq_col)


def _ssm_body(u_ref, wb_ref, wc_ref, ar_ref, ai_ref, d_ref, wg_ref, bg_ref, o_ref,
              st_ref, hs_ref, y_ref, *, n_tiles):
    bsz, tl, width = u_ref.shape
    rows = bsz * tl
    half_tiles = n_tiles // 2
    cols = wb_ref.shape[2] // LANES
    cplx = cols // 2

    @pl.when(pl.program_id(0) == 0)
    def _():
        hs_ref[...] = jnp.zeros_like(hs_ref)

    u2 = u_ref[...].reshape(rows, width)
    for i in range(n_tiles):
        half, j = divmod(i, half_tiles)
        bu = jnp.dot(u2[:, i * LANES:(i + 1) * LANES], wb_ref[i], preferred_element_type=F32)
        for k in range(cols):
            st_ref[j * cols + k, half * rows:(half + 1) * rows, :] = bu[:, k * LANES:(k + 1) * LANES]

    n_sub = 2 * bsz
    for j in range(half_tiles):
        re_cols = [j * cols + k for k in range(cplx)]
        im_cols = [j * cols + cplx + k for k in range(cplx)]
        ar = [ar_ref[j * cplx + k] for k in range(cplx)]
        ai = [ai_ref[j * cplx + k] for k in range(cplx)]

        def step(t, carry, re_cols=re_cols, im_cols=im_cols, ar=ar, ai=ai):
            hr, hi = carry
            rows_t = pl.ds(t, n_sub, stride=tl)
            nhr, nhi = [], []
            for k in range(cplx):
                r = ar[k] * hr[k] - ai[k] * hi[k] + st_ref[re_cols[k], rows_t, :]
                i_ = ar[k] * hi[k] + ai[k] * hr[k] + st_ref[im_cols[k], rows_t, :]
                st_ref[re_cols[k], rows_t, :] = r
                st_ref[im_cols[k], rows_t, :] = i_
                nhr.append(r)
                nhi.append(i_)
            return tuple(nhr), tuple(nhi)

        init = (tuple(hs_ref[c] for c in re_cols), tuple(hs_ref[c] for c in im_cols))
        hr, hi = lax.fori_loop(0, tl, step, init, unroll=8)
        for k in range(cplx):
            hs_ref[re_cols[k]] = hr[k]
            hs_ref[im_cols[k]] = hi[k]

    for i in range(n_tiles):
        half, j = divmod(i, half_tiles)
        hblk = jnp.concatenate([st_ref[j * cols + k, half * rows:(half + 1) * rows, :] for k in range(cols)],
                               axis=1).astype(BF16)
        y_ref[:, i * LANES:(i + 1) * LANES] = jnp.dot(hblk, wc_ref[i], preferred_element_type=F32)

    y = y_ref[...] + d_ref[...] * u2.astype(F32)
    y = 0.5 * y * (1.0 + lax.erf(y * (2.0 ** -0.5)))
    z = jnp.dot(y.astype(BF16), wg_ref[...], preferred_element_type=F32) + bg_ref[...]
    o_ref[...] = (y * jax.nn.sigmoid(z)).reshape(bsz, tl, width).astype(BF16)


def _ssm_call(u, wb, wc, ar, ai, d_row, w_glu, b_glu, *, tl):
    bsz, seq, width = u.shape
    n_tiles = wb.shape[0]
    state_cols = (n_tiles // 2) * wb.shape[2] // LANES
    body = functools.partial(_ssm_body, n_tiles=n_tiles)
    return pl.pallas_call(
        body,
        grid=(seq // tl,),
        in_specs=[pl.BlockSpec((bsz, tl, width), lambda c: (0, c, 0)),
                  _const_spec(wb.shape), _const_spec(wc.shape),
                  _const_spec(ar.shape), _const_spec(ai.shape),
                  _const_spec(d_row.shape), _const_spec(w_glu.shape), _const_spec(b_glu.shape)],
        out_specs=pl.BlockSpec((bsz, tl, width), lambda c: (0, c, 0)),
        out_shape=jax.ShapeDtypeStruct((bsz, seq, width), BF16),
        scratch_shapes=[pltpu.VMEM((state_cols, 2 * bsz * tl, LANES), F32),
                        pltpu.VMEM((state_cols, 2 * bsz, LANES), F32),
                        pltpu.VMEM((bsz * tl, width), F32)],
        compiler_params=_params("arbitrary"),
        name="ssm",
    )(u, wb, wc, ar, ai, d_row, w_glu, b_glu)


def _ssm_weights(lam_re, lam_im, b_re, b_im, c_re, c_im, log_step, bsz):
    n_groups, n_state = lam_re.shape
    gpt = LANES // b_re.shape[2]
    n_tiles = n_groups // gpt
    lr, li = lam_re.astype(F32), lam_im.astype(F32)
    dt = jnp.exp(log_step.astype(F32))[:, None]
    mag = jnp.exp(dt * lr)
    ar = mag * jnp.cos(dt * li)
    ai = mag * jnp.sin(dt * li)
    den = lr * lr + li * li
    zr = ar - 1.0
    kr = (zr * lr + ai * li) / den
    ki = (ai * lr - zr * li) / den
    br, bi = b_re.astype(F32), b_im.astype(F32)
    bbar_r = kr[..., None] * br - ki[..., None] * bi
    bbar_i = kr[..., None] * bi + ki[..., None] * br
    eye = jnp.eye(gpt, dtype=F32)

    def diag_in(w):
        w = w.reshape(n_tiles, gpt, n_state, -1).transpose(0, 1, 3, 2)
        w = w[:, :, :, None, :] * eye[None, :, None, :, None]
        return w.reshape(n_tiles, gpt * w.shape[2], gpt * n_state)

    def diag_out(w):
        w = w.reshape(n_tiles, gpt, -1, n_state).transpose(0, 1, 3, 2)
        w = w[:, :, :, None, :] * eye[None, :, None, :, None]
        return w.reshape(n_tiles, gpt * n_state, gpt * w.shape[4])

    wb = jnp.concatenate([diag_in(bbar_r), diag_in(bbar_i)], axis=2).astype(BF16)
    wc = jnp.concatenate([diag_out(c_re.astype(F32)), diag_out(-c_im.astype(F32))], axis=1).astype(BF16)
    scan = lambda a: jnp.repeat(a.reshape(2, -1, LANES).transpose(1, 0, 2), bsz, axis=1)
    return wb, wc, scan(ar), scan(ai)


def _attn_body(q_ref, k_ref, v_ref, l1_ref, l2_ref, l3_ref, l4_ref, gs_ref, o_ref, *, lambda_init, qk_dim):
    qi = pl.program_id(2)
    qt = q_ref[0, 0]
    tq = qt.shape[1]
    tk = tq
    dv = v_ref.shape[2]
    zero = jnp.zeros((qk_dim, tq), qt.dtype)
    qbd = jnp.concatenate([jnp.concatenate([qt[:qk_dim], zero], axis=1),
                           jnp.concatenate([zero, qt[qk_dim:]], axis=1)], axis=0)

    def update(j, carry, masked):
        m, l, acc = carry
        kb = k_ref[0, pl.ds(pl.multiple_of(j * tk, tk), tk), :]
        s = jnp.dot(kb, qbd, preferred_element_type=F32)
        if masked:
            key_chunk = lax.broadcasted_iota(jnp.int32, s.shape, 0) // CHUNK
            qry_chunk = (lax.broadcasted_iota(jnp.int32, s.shape, 1) % tq) // CHUNK
            s = jnp.where(key_chunk <= qry_chunk, s, -jnp.inf)
        m_new = jnp.maximum(m, jnp.max(s, axis=0, keepdims=True))
        alpha = jnp.exp(m - m_new)
        p = jnp.exp(s - m_new)
        l = alpha * l + jnp.sum(p, axis=0, keepdims=True)
        pv = jnp.dot(v_ref[0, j], p.astype(BF16), preferred_element_type=F32)
        return m_new, l, alpha * acc + pv

    init = (jnp.full((1, 2 * tq), -jnp.inf, F32), jnp.zeros((1, 2 * tq), F32), jnp.zeros((dv, 2 * tq), F32))
    carry = lax.fori_loop(0, qi, functools.partial(update, masked=False), init)
    _, l, acc = update(qi, carry, True)

    lam = (jnp.exp(jnp.sum(l1_ref[...] * l2_ref[...], axis=1, keepdims=True))
           - jnp.exp(jnp.sum(l3_ref[...] * l4_ref[...], axis=1, keepdims=True)) + lambda_init)
    o = acc[:, :tq] / l[:, :tq] - lam * (acc[:, tq:] / l[:, tq:])
    ms = jnp.mean(o * o, axis=0, keepdims=True)
    o = o * lax.rsqrt(ms + RMS_EPS) * gs_ref[...] * (1.0 - lambda_init)
    o_ref[0] = o.T.astype(BF16)


def _attn_call(qt, k, vt, lq1, lk1, lq2, lk2, gs_col, *, lambda_init, qk_dim):
    bsz, nq, width, tq = qt.shape
    seq = k.shape[1]
    dv = gs_col.shape[0]
    heads = width // dv
    body = functools.partial(_attn_body, lambda_init=lambda_init, qk_dim=qk_dim)
    vec = _const_spec(lq1.shape)
    return pl.pallas_call(
        body,
        grid=(bsz, heads, nq),
        in_specs=[pl.BlockSpec((1, 1, dv, tq), lambda b, h, i: (b, i, h, 0)),
                  pl.BlockSpec((1, seq, dv), lambda b, h, i: (b, 0, h)),
                  pl.BlockSpec((1, nq, dv, tq), lambda b, h, i: (b, 0, h, 0)),
                  vec, vec, vec, vec, _const_spec(gs_col.shape)],
        out_specs=pl.BlockSpec((1, tq, dv), lambda b, h, i: (b, i, h)),
        out_shape=jax.ShapeDtypeStruct((bsz, seq, width), BF16),
        compiler_params=_params("parallel", "parallel", "arbitrary"),
        name="attn",
    )(qt, k, vt, lq1, lk1, lq2, lk2, gs_col)


def _outproj_body(x_ref, ys_ref, ya_ref, w_ref, gate_ref, g_ref, sc_ref, sh_ref, x1_ref, h2_ref):
    half = ys_ref.shape[2]
    mixed = (jnp.dot(ys_ref[0], w_ref[:half], preferred_element_type=F32)
             + jnp.dot(ya_ref[0], w_ref[half:], preferred_element_type=F32))
    x1 = x_ref[0] + gate_ref[0] * mixed
    x1_ref[0] = x1
    h2_ref[0] = _modulated_norm(x1, g_ref[...], sc_ref[0], sh_ref[0]).astype(BF16)


def _outproj_call(x, y_ssm, y_att, w_out, gate, g, scale, shift, *, tm):
    bsz, seq, d = x.shape
    half = y_ssm.shape[2]
    tok = lambda b, i: (b, i, 0)
    per_b = lambda b, i: (b, 0, 0)
    return pl.pallas_call(
        _outproj_body,
        grid=(bsz, seq // tm),
        in_specs=[pl.BlockSpec((1, tm, d), tok),
                  pl.BlockSpec((1, tm, half), tok),
                  pl.BlockSpec((1, tm, half), tok),
                  _const_spec(w_out.shape),
                  pl.BlockSpec((1, 1, d), per_b),
                  _const_spec((1, d)),
                  pl.BlockSpec((1, 1, d), per_b),
                  pl.BlockSpec((1, 1, d), per_b)],
        out_specs=[pl.BlockSpec((1, tm, d), tok), pl.BlockSpec((1, tm, d), tok)],
        out_shape=[jax.ShapeDtypeStruct((bsz, seq, d), F32), jax.ShapeDtypeStruct((bsz, seq, d), BF16)],
        compiler_params=_params("parallel", "parallel"),
        name="outproj",
    )(x, y_ssm, y_att, w_out, gate, g, scale, shift)


def _mlp_body(h_ref, x_ref, w1_ref, w2_ref, gate_ref, o_ref):
    kk = pl.program_id(2)
    a = jnp.maximum(jnp.dot(h_ref[0], w1_ref[...], preferred_element_type=F32), 0.0)
    part = jnp.dot((a * a).astype(BF16), w2_ref[...], preferred_element_type=F32)

    @pl.when(kk == 0)
    def _():
        o_ref[0] = part

    @pl.when(kk > 0)
    def _():
        o_ref[0] += part

    @pl.when(kk == pl.num_programs(2) - 1)
    def _():
        o_ref[0] = x_ref[0] + gate_ref[0] * o_ref[0]


def _mlp_call(h2, x1, w1, w2, gate, *, tm, th):
    bsz, seq, d = x1.shape
    hidden = w1.shape[1]
    tok = lambda b, i, k: (b, i, 0)
    return pl.pallas_call(
        _mlp_body,
        grid=(bsz, seq // tm, hidden // th),
        in_specs=[pl.BlockSpec((1, tm, d), tok),
                  pl.BlockSpec((1, tm, d), tok),
                  pl.BlockSpec((d, th), lambda b, i, k: (0, k)),
                  pl.BlockSpec((th, d), lambda b, i, k: (k, 0)),
                  pl.BlockSpec((1, 1, d), lambda b, i, k: (b, 0, 0))],
        out_specs=pl.BlockSpec((1, tm, d), tok),
        out_shape=jax.ShapeDtypeStruct((bsz, seq, d), F32),
        compiler_params=_params("parallel", "parallel", "arbitrary"),
        name="mlp",
    )(h2, x1, w1, w2, gate)


def _tile(n, target):
    t = min(n, target)
    assert n % t == 0, (n, target)
    return t


def kernel(x, c, w_ada, b_ada, g_norm_mix, g_norm_mlp, w_in, ssm_lambda_re, ssm_lambda_im, ssm_b_re, ssm_b_im, ssm_c_re, ssm_c_im, ssm_d, ssm_log_step, w_glu, b_glu, g_q, g_k, lambda_q1, lambda_k1, lambda_q2, lambda_k2, g_subln, w_out, w_mlp1, w_mlp2):
    bsz, seq, d = x.shape
    depth = w_ada.shape[0]
    ssm_width = w_glu.shape[1]
    qk_dim = g_q.shape[1]
    dv = g_subln.shape[1]
    attn_width = (w_in.shape[2] - ssm_width) // 3
    n_sub = attn_width // qk_dim
    assert dv == 2 * qk_dim and ssm_width == attn_width

    tm = _tile(seq, 512)
    tq = _tile(seq, 512)
    tl = _tile(seq, 128)
    th = _tile(w_mlp1.shape[2], 1024)

    seg = (jnp.arange(MXU_DIM)[:, None] // qk_dim == jnp.arange(MXU_DIM)[None, :] // qk_dim).astype(BF16)
    row = lambda v: v.reshape(1, -1).astype(F32)

    for l in range(depth):
        lambda_init = 0.8 - 0.6 * math.exp(-0.3 * l)
        mod = _ada_call(c, w_ada[l], b_ada[l])
        shift1, scale1, gate1, shift2, scale2, gate2 = [m.reshape(bsz, 1, d) for m in jnp.split(mod, 6, axis=-1)]

        wl = w_in[l]
        u_w, q_w, k_w, v_w = (wl[:, :ssm_width], wl[:, ssm_width:ssm_width + attn_width],
                              wl[:, ssm_width + attn_width:ssm_width + 2 * attn_width],
                              wl[:, ssm_width + 2 * attn_width:])
        w_uk = jnp.concatenate([u_w, k_w], axis=1).astype(BF16)
        w_qvt = jnp.concatenate([q_w, v_w], axis=1).T.astype(BF16)
        gk_row = jnp.tile(g_k[l].astype(F32), n_sub).reshape(1, attn_width)
        gq_col = (jnp.tile(g_q[l].astype(F32), n_sub) * (qk_dim ** -0.5)).reshape(attn_width, 1)
        u, k_hat, q_t, v_t = _inproj_call(x, row(g_norm_mix[l]), scale1, shift1, w_uk, w_qvt, seg, gk_row, gq_col,
                                          qk_dim=qk_dim, tm=tm, tq=tq)

        wb, wc, ar, ai = _ssm_weights(ssm_lambda_re[l], ssm_lambda_im[l], ssm_b_re[l], ssm_b_im[l],
                                      ssm_c_re[l], ssm_c_im[l], ssm_log_step[l], bsz)
        y_ssm = _ssm_call(u, wb, wc, ar, ai, row(ssm_d[l]), w_glu[l].astype(BF16), row(b_glu[l]), tl=tl)

        y_att = _attn_call(q_t, k_hat, v_t, row(lambda_q1[l]), row(lambda_k1[l]), row(lambda_q2[l]),
                           row(lambda_k2[l]), g_subln[l].astype(F32).reshape(dv, 1),
                           lambda_init=lambda_init, qk_dim=qk_dim)

        x1, h2 = _outproj_call(x, y_ssm, y_att, w_out[l].astype(BF16), gate1, row(g_norm_mlp[l]), scale2, shift2, tm=tm)
        x = _mlp_call(h2, x1, w_mlp1[l].astype(BF16), w_mlp2[l].astype(BF16), gate2, tm=tm, th=th)
    return x
```

```python
import functools
import math

import jax
import jax.numpy as jnp
from jax import lax
from jax.experimental import pallas as pl
from jax.experimental.pallas import tpu as pltpu

CHUNK = 64
RMS_EPS = 1e-6
LANES = 128
SUBLANES = 8
MXU_DIM = 256
VMEM_LIMIT = 56 * 1024 * 1024

F32 = jnp.float32
BF16 = jnp.bfloat16


def _params(*sem):
    return pltpu.CompilerParams(dimension_semantics=sem, vmem_limit_bytes=VMEM_LIMIT)


def _const_spec(shape):
    nd = len(shape)
    return pl.BlockSpec(shape, lambda *_: (0,) * nd, pipeline_mode=pl.Buffered(1))


def _ada_body(c_ref, w_ref, b_ref, o_ref):
    c = c_ref[...]
    c_act = c * jax.nn.sigmoid(c)
    o_ref[...] = jnp.dot(c_act, w_ref[...], preferred_element_type=F32,
                         precision=lax.Precision.HIGHEST) + b_ref[...]


def _ada_call(c, w_ada, b_ada, tn=1024):
    bsz, d = c.shape
    n = w_ada.shape[1]
    return pl.pallas_call(
        _ada_body,
        grid=(n // tn,),
        in_specs=[pl.BlockSpec((bsz, d), lambda j: (0, 0)),
                  pl.BlockSpec((d, tn), lambda j: (0, j)),
                  pl.BlockSpec((1, tn), lambda j: (0, j))],
        out_specs=pl.BlockSpec((bsz, tn), lambda j: (0, j)),
        out_shape=jax.ShapeDtypeStruct((bsz, n), F32),
        compiler_params=_params("parallel"),
        name="ada",
    )(c, w_ada, b_ada.reshape(1, n))


def _modulated_norm(x, g, scale, shift):
    ms = jnp.mean(x * x, axis=-1, keepdims=True)
    return x * lax.rsqrt(ms + RMS_EPS) * g * (1.0 + scale) + shift


def _inproj_body(x_ref, g_ref, sc_ref, sh_ref, wuk_ref, wqv_ref, seg_ref, gk_ref, gq_ref,
                 u_ref, k_ref, q_ref, v_ref, *, width, qk_dim, tq):
    x = x_ref[0]
    h = _modulated_norm(x, g_ref[...], sc_ref[0], sh_ref[0]).astype(BF16)
    uk = jnp.dot(h, wuk_ref[...], preferred_element_type=F32)
    u_ref[0] = uk[:, :width].astype(BF16)
    k = uk[:, width:]
    parts = []
    for t in range(width // MXU_DIM):
        kt = k[:, t * MXU_DIM:(t + 1) * MXU_DIM]
        ssq = jnp.dot((kt * kt).astype(BF16), seg_ref[...], preferred_element_type=F32)
        parts.append(kt * lax.rsqrt(ssq * (1.0 / qk_dim) + RMS_EPS))
    k_ref[0] = (jnp.concatenate(parts, axis=1) * gk_ref[...]).astype(BF16)
    qv = lax.dot_general(wqv_ref[...], h, (((1,), (1,)), ((), ())), preferred_element_type=F32)
    tm = qv.shape[1]
    q3 = qv[:width].reshape(width // qk_dim, qk_dim, tm)
    ssq = jnp.sum(q3 * q3, axis=1, keepdims=True)
    qn = (q3 * lax.rsqrt(ssq * (1.0 / qk_dim) + RMS_EPS)).reshape(width, tm) * gq_ref[...]
    vt = qv[width:]
    for r in range(tm // tq):
        q_ref[0, r] = qn[:, r * tq:(r + 1) * tq].astype(BF16)
        v_ref[0, r] = vt[:, r * tq:(r + 1) * tq].astype(BF16)


def _inproj_call(x, g, scale, shift, w_uk, w_qvt, seg, gk_row, gq_col, *, qk_dim, tm, tq):
    bsz, seq, d = x.shape
    width = w_uk.shape[1] // 2
    nq = seq // tq
    r = tm // tq
    tok = lambda b, i: (b, i, 0)
    per_b = lambda b, i: (b, 0, 0)
    body = functools.partial(_inproj_body, width=width, qk_dim=qk_dim, tq=tq)
    return pl.pallas_call(
        body,
        grid=(bsz, seq // tm),
        in_specs=[pl.BlockSpec((1, tm, d), tok),
                  _const_spec((1, d)),
                  pl.BlockSpec((1, 1, d), per_b),
                  pl.BlockSpec((1, 1, d), per_b),
                  _const_spec(w_uk.shape),
                  _const_spec(w_qvt.shape),
                  _const_spec(seg.shape),
                  _const_spec(gk_row.shape),
                  _const_spec(gq_col.shape)],
        out_specs=[pl.BlockSpec((1, tm, width), tok),
                   pl.BlockSpec((1, tm, width), tok),
                   pl.BlockSpec((1, r, width, tq), lambda b, i: (b, i, 0, 0)),
                   pl.BlockSpec((1, r, width, tq), lambda b, i: (b, i, 0, 0))],
        out_shape=[jax.ShapeDtypeStruct((bsz, seq, width), BF16),
                   jax.ShapeDtypeStruct((bsz, seq, width), BF16),
                   jax.ShapeDtypeStruct((bsz, nq, width, tq), BF16),
                   jax.ShapeDtypeStruct((bsz, nq, width, tq), BF16)],
        compiler_params=_params("parallel", "parallel"),
        name="inproj",
    )(x, g, scale, shift, w_uk, w_qvt, seg, gk_row, gq_col)


def _ssm_body(u_ref, wb_ref, wc_ref, coef_ref, d_ref, wg_ref, bg_ref, o_ref,
              st_ref, hs_ref, y_ref, *, n_tiles):
    bsz, tl, width = u_ref.shape
    rows = bsz * tl
    cols = wb_ref.shape[2] // LANES
    cplx = cols // 2

    @pl.when(pl.program_id(0) == 0)
    def _():
        hs_ref[...] = jnp.zeros_like(hs_ref)

    u2 = u_ref[...].reshape(rows, width)
    for i in range(n_tiles):
        bu = jnp.dot(u2[:, i * LANES:(i + 1) * LANES], wb_ref[i], preferred_element_type=F32)
        for k in range(cols):
            st_ref[i * cols + k] = bu[:, k * LANES:(k + 1) * LANES]

    def column(cp, _):
        re_col = (cp // cplx) * cols + cp % cplx
        im_col = re_col + cplx
        a1r, a1i, a2r, a2i, a4r, a4i, pwr, pwi = [coef_ref[cp, n] for n in range(8)]

        def cmul_add(xr, xi, cr, ci, sr, si):
            return xr + cr * sr - ci * si, xi + cr * si + ci * sr

        def group(g, carry):
            out = []
            for b in range(bsz):
                hr0, hi0 = carry[b]
                r8 = pl.ds(pl.multiple_of(b * tl + g * SUBLANES, SUBLANES), SUBLANES)
                xr, xi = st_ref[re_col, r8, :], st_ref[im_col, r8, :]
                xr, xi = cmul_add(xr, xi, a1r, a1i, pltpu.roll(xr, 1, 0), pltpu.roll(xi, 1, 0))
                xr, xi = cmul_add(xr, xi, a2r, a2i, pltpu.roll(xr, 2, 0), pltpu.roll(xi, 2, 0))
                xr, xi = cmul_add(xr, xi, a4r, a4i, pltpu.roll(xr, 4, 0), pltpu.roll(xi, 4, 0))
                xr, xi = cmul_add(xr, xi, pwr, pwi, hr0, hi0)
                st_ref[re_col, r8, :] = xr
                st_ref[im_col, r8, :] = xi
                last = slice(SUBLANES - 1, SUBLANES)
                out.append((jnp.broadcast_to(xr[last], xr.shape), jnp.broadcast_to(xi[last], xi.shape)))
            return tuple(out)

        init = tuple((hs_ref[cp, b, 0], hs_ref[cp, b, 1]) for b in range(bsz))
        fin = lax.fori_loop(0, tl // SUBLANES, group, init)
        for b in range(bsz):
            hs_ref[cp, b, 0] = fin[b][0]
            hs_ref[cp, b, 1] = fin[b][1]
        return 0

    lax.fori_loop(0, n_tiles * cplx, column, 0)

    for i in range(n_tiles):
        hblk = jnp.concatenate([st_ref[i * cols + k] for k in range(cols)], axis=1).astype(BF16)
        y_ref[:, i * LANES:(i + 1) * LANES] = jnp.dot(hblk, wc_ref[i], preferred_element_type=F32)

    y = y_ref[...] + d_ref[...] * u2.astype(F32)
    y = 0.5 * y * (1.0 + lax.erf(y * (2.0 ** -0.5)))
    z = jnp.dot(y.astype(BF16), wg_ref[...], preferred_element_type=F32) + bg_ref[...]
    o_ref[...] = (y * jax.nn.sigmoid(z)).reshape(bsz, tl, width).astype(BF16)


def _ssm_call(u, wb, wc, coef, d_row, w_glu, b_glu, *, tl):
    bsz, seq, width = u.shape
    n_tiles = wb.shape[0]
    state_cols = n_tiles * wb.shape[2] // LANES
    body = functools.partial(_ssm_body, n_tiles=n_tiles)
    return pl.pallas_call(
        body,
        grid=(seq // tl,),
        in_specs=[pl.BlockSpec((bsz, tl, width), lambda c: (0, c, 0)),
                  _const_spec(wb.shape), _const_spec(wc.shape), _const_spec(coef.shape),
                  _const_spec(d_row.shape), _const_spec(w_glu.shape), _const_spec(b_glu.shape)],
        out_specs=pl.BlockSpec((bsz, tl, width), lambda c: (0, c, 0)),
        out_shape=jax.ShapeDtypeStruct((bsz, seq, width), BF16),
        scratch_shapes=[pltpu.VMEM((state_cols, bsz * tl, LANES), F32),
                        pltpu.VMEM((state_cols // 2, bsz, 2, SUBLANES, LANES), F32),
                        pltpu.VMEM((bsz * tl, width), F32)],
        compiler_params=_params("arbitrary"),
        name="ssm",
    )(u, wb, wc, coef, d_row, w_glu, b_glu)


def _ssm_weights(lam_re, lam_im, b_re, b_im, c_re, c_im, log_step):
    n_groups, n_state = lam_re.shape
    gpt = LANES // b_re.shape[2]
    n_tiles = n_groups // gpt
    lr, li = lam_re.astype(F32), lam_im.astype(F32)
    dt = jnp.exp(log_step.astype(F32))[:, None]
    mag = jnp.exp(dt * lr)
    ar = mag * jnp.cos(dt * li)
    ai = mag * jnp.sin(dt * li)
    den = lr * lr + li * li
    zr = ar - 1.0
    kr = (zr * lr + ai * li) / den
    ki = (ai * lr - zr * li) / den
    br, bi = b_re.astype(F32), b_im.astype(F32)
    bbar_r = kr[..., None] * br - ki[..., None] * bi
    bbar_i = kr[..., None] * bi + ki[..., None] * br
    eye = jnp.eye(gpt, dtype=F32)

    def diag_in(w):
        w = w.reshape(n_tiles, gpt, n_state, -1).transpose(0, 1, 3, 2)
        w = w[:, :, :, None, :] * eye[None, :, None, :, None]
        return w.reshape(n_tiles, gpt * w.shape[2], gpt * n_state)

    def diag_out(w):
        w = w.reshape(n_tiles, gpt, -1, n_state).transpose(0, 1, 3, 2)
        w = w[:, :, :, None, :] * eye[None, :, None, :, None]
        return w.reshape(n_tiles, gpt * n_state, gpt * w.shape[4])

    wb = jnp.concatenate([diag_in(bbar_r), diag_in(bbar_i)], axis=2).astype(BF16)
    wc = jnp.concatenate([diag_out(c_re.astype(F32)), diag_out(-c_im.astype(F32))], axis=1).astype(BF16)
    steps = jnp.arange(1, SUBLANES + 1, dtype=F32)[:, None, None]
    pmag = jnp.exp(steps * dt * lr)
    pw_r = (pmag * jnp.cos(steps * dt * li)).reshape(SUBLANES, -1, LANES).transpose(1, 0, 2)
    pw_i = (pmag * jnp.sin(steps * dt * li)).reshape(SUBLANES, -1, LANES).transpose(1, 0, 2)
    sub = jnp.arange(SUBLANES)[None, :, None]
    coef = []
    for k in (1, 2, 4):
        coef += [jnp.where(sub >= k, pw_r[:, k - 1:k], 0.0), jnp.where(sub >= k, pw_i[:, k - 1:k], 0.0)]
    coef += [pw_r, pw_i]
    return wb, wc, jnp.stack(coef, axis=1)


def _attn_body(q_ref, k_ref, v_ref, l1_ref, l2_ref, l3_ref, l4_ref, gs_ref, o_ref, acc_ref, sa_ref, sb_ref,
               *, lambda_init, qk_dim):
    qi = pl.program_id(2)
    qt = q_ref[0, 0]
    tq = qt.shape[1]
    tk = tq
    dv = v_ref.shape[2]
    zero = jnp.zeros((qk_dim, tq), qt.dtype)
    qbd = jnp.concatenate([jnp.concatenate([qt[:qk_dim], zero], axis=1),
                           jnp.concatenate([zero, qt[qk_dim:]], axis=1)], axis=0)

    sw = min(tq, MXU_DIM)
    n_str = 2 * tq // sw

    def score_into(j, dst_ref):
        kb = k_ref[0, pl.ds(pl.multiple_of(j * tk, tk), tk), :]
        s = jnp.dot(kb, qbd, preferred_element_type=F32)
        dst_ref[...] = s
        return jnp.max(s, axis=0, keepdims=True)

    def consume(j, src_ref, blk_max, m, l, masked):
        vb = v_ref[0, j]
        ps, alphas, ms, ls = [], [], [], []
        for c in range(n_str):
            cols = slice(c * sw, (c + 1) * sw)
            q0 = (c * sw) % tq
            nk = min(tk, q0 + sw) if masked else tk
            s_c = src_ref[:nk, cols]
            if masked:
                key_chunk = lax.broadcasted_iota(jnp.int32, s_c.shape, 0) // CHUNK
                qry_chunk = (lax.broadcasted_iota(jnp.int32, s_c.shape, 1) + q0) // CHUNK
                s_c = jnp.where(key_chunk <= qry_chunk, s_c, -jnp.inf)
                blk_max_c = jnp.max(s_c, axis=0, keepdims=True)
            else:
                blk_max_c = blk_max[:, cols]
            m_new = jnp.maximum(m[:, cols], blk_max_c)
            alpha = jnp.exp2(m[:, cols] - m_new)
            p = jnp.exp2(s_c - m_new)
            ls.append(alpha * l[:, cols] + jnp.sum(p, axis=0, keepdims=True))
            ms.append(m_new)
            alphas.append(alpha)
            ps.append(p.astype(BF16))
        for c in range(n_str):
            cols = slice(c * sw, (c + 1) * sw)
            nk = ps[c].shape[0]
            pv = jnp.dot(vb[:, :nk], ps[c], preferred_element_type=F32)
            acc_ref[:, cols] = alphas[c] * acc_ref[:, cols] + pv
        return jnp.concatenate(ms, axis=1), jnp.concatenate(ls, axis=1)

    def pair(i, carry):
        m, l, max_a = carry
        max_b = score_into(2 * i + 1, sb_ref)
        m, l = consume(2 * i, sa_ref, max_a, m, l, False)
        max_a = score_into(2 * i + 2, sa_ref)
        m, l = consume(2 * i + 1, sb_ref, max_b, m, l, False)
        return m, l, max_a

    def odd_tail(carry):
        m, l, max_a = carry
        score_into(qi, sb_ref)
        m, l = consume(qi - 1, sa_ref, max_a, m, l, False)
        return consume(qi, sb_ref, None, m, l, True)[1]

    def even_tail(carry):
        m, l, _ = carry
        return consume(qi, sa_ref, None, m, l, True)[1]

    acc_ref[...] = jnp.zeros_like(acc_ref)
    init = (jnp.full((1, 2 * tq), -jnp.inf, F32), jnp.zeros((1, 2 * tq), F32), score_into(0, sa_ref))
    carry = lax.fori_loop(0, qi // 2, pair, init)
    l = lax.cond(qi % 2 == 1, odd_tail, even_tail, carry)
    acc = acc_ref[...]

    lam = (jnp.exp(jnp.sum(l1_ref[...] * l2_ref[...], axis=1, keepdims=True))
           - jnp.exp(jnp.sum(l3_ref[...] * l4_ref[...], axis=1, keepdims=True)) + lambda_init)
    o = acc[:, :tq] / l[:, :tq] - lam * (acc[:, tq:] / l[:, tq:])
    ms = jnp.mean(o * o, axis=0, keepdims=True)
    o = o * lax.rsqrt(ms + RMS_EPS) * gs_ref[...] * (1.0 - lambda_init)
    o_ref[0] = o.T.astype(BF16)


def _attn_call(qt, k, vt, lq1, lk1, lq2, lk2, gs_col, *, lambda_init, qk_dim):
    bsz, nq, width, tq = qt.shape
    seq = k.shape[1]
    dv = gs_col.shape[0]
    heads = width // dv
    body = functools.partial(_attn_body, lambda_init=lambda_init, qk_dim=qk_dim)
    vec = _const_spec(lq1.shape)
    return pl.pallas_call(
        body,
        grid=(bsz, heads, nq),
        in_specs=[pl.BlockSpec((1, 1, dv, tq), lambda b, h, i: (b, i, h, 0)),
                  pl.BlockSpec((1, seq, dv), lambda b, h, i: (b, 0, h)),
                  pl.BlockSpec((1, nq, dv, tq), lambda b, h, i: (b, 0, h, 0)),
                  vec, vec, vec, vec, _const_spec(gs_col.shape)],
        out_specs=pl.BlockSpec((1, tq, dv), lambda b, h, i: (b, i, h)),
        out_shape=jax.ShapeDtypeStruct((bsz, seq, width), BF16),
        scratch_shapes=[pltpu.VMEM((dv, 2 * tq), F32),
                        pltpu.VMEM((tq, 2 * tq), F32),
                        pltpu.VMEM((tq, 2 * tq), F32)],
        compiler_params=_params("parallel", "parallel", "arbitrary"),
        name="attn",
    )(qt, k, vt, lq1, lk1, lq2, lk2, gs_col)


def _outproj_body(x_ref, ys_ref, ya_ref, w_ref, gate_ref, g_ref, sc_ref, sh_ref, x1_ref, h2_ref):
    half = ys_ref.shape[2]
    mixed = (jnp.dot(ys_ref[0], w_ref[:half], preferred_element_type=F32)
             + jnp.dot(ya_ref[0], w_ref[half:], preferred_element_type=F32))
    x1 = x_ref[0] + gate_ref[0] * mixed
    x1_ref[0] = x1
    h2_ref[0] = _modulated_norm(x1, g_ref[...], sc_ref[0], sh_ref[0]).astype(BF16)


def _outproj_call(x, y_ssm, y_att, w_out, gate, g, scale, shift, *, tm):
    bsz, seq, d = x.shape
    half = y_ssm.shape[2]
    tok = lambda b, i: (b, i, 0)
    per_b = lambda b, i: (b, 0, 0)
    return pl.pallas_call(
        _outproj_body,
        grid=(bsz, seq // tm),
        in_specs=[pl.BlockSpec((1, tm, d), tok),
                  pl.BlockSpec((1, tm, half), tok),
                  pl.BlockSpec((1, tm, half), tok),
                  _const_spec(w_out.shape),
                  pl.BlockSpec((1, 1, d), per_b),
                  _const_spec((1, d)),
                  pl.BlockSpec((1, 1, d), per_b),
                  pl.BlockSpec((1, 1, d), per_b)],
        out_specs=[pl.BlockSpec((1, tm, d), tok), pl.BlockSpec((1, tm, d), tok)],
        out_shape=[jax.ShapeDtypeStruct((bsz, seq, d), F32), jax.ShapeDtypeStruct((bsz, seq, d), BF16)],
        compiler_params=_params("parallel", "parallel"),
        name="outproj",
    )(x, y_ssm, y_att, w_out, gate, g, scale, shift)


def _mlp_body(h_ref, x_ref, w1_ref, w2_ref, gate_ref, o_ref):
    kk = pl.program_id(2)

    @pl.when(kk == 0)
    def _():
        o_ref[...] = jnp.zeros_like(o_ref)

    a = jnp.maximum(jnp.dot(h_ref[0], w1_ref[...], preferred_element_type=F32), 0.0)
    o_ref[0] += jnp.dot((a * a).astype(BF16), w2_ref[...], preferred_element_type=F32)

    @pl.when(kk == pl.num_programs(2) - 1)
    def _():
        o_ref[0] = x_ref[0] + gate_ref[0] * o_ref[0]


def _mlp_call(h2, x1, w1, w2, gate, *, tm, th):
    bsz, seq, d = x1.shape
    hidden = w1.shape[1]
    tok = lambda b, i, k: (b, i, 0)
    return pl.pallas_call(
        _mlp_body,
        grid=(bsz, seq // tm, hidden // th),
        in_specs=[pl.BlockSpec((1, tm, d), tok),
                  pl.BlockSpec((1, tm, d), tok),
                  pl.BlockSpec((d, th), lambda b, i, k: (0, k)),
                  pl.BlockSpec((th, d), lambda b, i, k: (k, 0)),
                  pl.BlockSpec((1, 1, d), lambda b, i, k: (b, 0, 0))],
        out_specs=pl.BlockSpec((1, tm, d), tok),
        out_shape=jax.ShapeDtypeStruct((bsz, seq, d), F32),
        compiler_params=_params("parallel", "parallel", "arbitrary"),
        name="mlp",
    )(h2, x1, w1, w2, gate)


def _tile(n, target):
    t = min(n, target)
    assert n % t == 0, (n, target)
    return t


def kernel(x, c, w_ada, b_ada, g_norm_mix, g_norm_mlp, w_in, ssm_lambda_re, ssm_lambda_im, ssm_b_re, ssm_b_im, ssm_c_re, ssm_c_im, ssm_d, ssm_log_step, w_glu, b_glu, g_q, g_k, lambda_q1, lambda_k1, lambda_q2, lambda_k2, g_subln, w_out, w_mlp1, w_mlp2):
    bsz, seq, d = x.shape
    depth = w_ada.shape[0]
    ssm_width = w_glu.shape[1]
    qk_dim = g_q.shape[1]
    dv = g_subln.shape[1]
    attn_width = (w_in.shape[2] - ssm_width) // 3
    n_sub = attn_width // qk_dim
    assert dv == 2 * qk_dim and ssm_width == attn_width

    tm = _tile(seq, 512)
    tq = _tile(seq, 512)
    tl = _tile(seq, 128)
    th = _tile(w_mlp1.shape[2], 1024)

    seg = (jnp.arange(MXU_DIM)[:, None] // qk_dim == jnp.arange(MXU_DIM)[None, :] // qk_dim).astype(BF16)
    row = lambda v: v.reshape(1, -1).astype(F32)

    for l in range(depth):
        lambda_init = 0.8 - 0.6 * math.exp(-0.3 * l)
        mod = _ada_call(c, w_ada[l], b_ada[l])
        shift1, scale1, gate1, shift2, scale2, gate2 = [m.reshape(bsz, 1, d) for m in jnp.split(mod, 6, axis=-1)]

        wl = w_in[l]
        u_w, q_w, k_w, v_w = (wl[:, :ssm_width], wl[:, ssm_width:ssm_width + attn_width],
                              wl[:, ssm_width + attn_width:ssm_width + 2 * attn_width],
                              wl[:, ssm_width + 2 * attn_width:])
        w_uk = jnp.concatenate([u_w, k_w], axis=1).astype(BF16)
        w_qvt = jnp.concatenate([q_w, v_w], axis=1).T.astype(BF16)
        gk_row = jnp.tile(g_k[l].astype(F32), n_sub).reshape(1, attn_width)
        gq_col = (jnp.tile(g_q[l].astype(F32), n_sub) * (qk_dim ** -0.5 * math.log2(math.e))).reshape(attn_width, 1)
        u, k_hat, q_t, v_t = _inproj_call(x, row(g_norm_mix[l]), scale1, shift1, w_uk, w_qvt, seg, gk_row, gq_col,
                                          qk_dim=qk_dim, tm=tm, tq=tq)

        wb, wc, coef = _ssm_weights(ssm_lambda_re[l], ssm_lambda_im[l], ssm_b_re[l], ssm_b_im[l],
                                    ssm_c_re[l], ssm_c_im[l], ssm_log_step[l])
        y_ssm = _ssm_call(u, wb, wc, coef, row(ssm_d[l]), w_glu[l].astype(BF16), row(b_glu[l]), tl=tl)

        y_att = _attn_call(q_t, k_hat, v_t, row(lambda_q1[l]), row(lambda_k1[l]), row(lambda_q2[l]),
                           row(lambda_k2[l]), g_subln[l].astype(F32).reshape(dv, 1),
                           lambda_init=lambda_init, qk_dim=qk_dim)

        x1, h2 = _outproj_call(x, y_ssm, y_att, w_out[l].astype(BF16), gate1, row(g_norm_mlp[l]), scale2, shift2, tm=tm)
        x = _mlp_call(h2, x1, w_mlp1[l].astype(BF16), w_mlp2[l].astype(BF16), gate2, tm=tm, th=th)
    return x
```

```python
import functools
import math

import jax
import jax.numpy as jnp
from jax import lax
from jax.experimental import pallas as pl
from jax.experimental.pallas import tpu as pltpu

CHUNK = 64
RMS_EPS = 1e-6
LANES = 128
SUBLANES = 8
MXU_DIM = 256
SUM_ROWS = 16
VMEM_LIMIT = 56 * 1024 * 1024

F32 = jnp.float32
BF16 = jnp.bfloat16


def _params(*sem):
    return pltpu.CompilerParams(dimension_semantics=sem, vmem_limit_bytes=VMEM_LIMIT)


def _const_spec(shape):
    nd = len(shape)
    return pl.BlockSpec(shape, lambda *_: (0,) * nd, pipeline_mode=pl.Buffered(1))


def _ada_body(c_ref, w_ref, b_ref, o_ref):
    c = c_ref[...]
    c_act = c * jax.nn.sigmoid(c)
    o_ref[...] = jnp.dot(c_act, w_ref[...], preferred_element_type=F32,
                         precision=lax.Precision.HIGHEST) + b_ref[...]


def _ada_call(c, w_ada, b_ada, tn=1024):
    bsz, d = c.shape
    n = w_ada.shape[1]
    return pl.pallas_call(
        _ada_body,
        grid=(n // tn,),
        in_specs=[pl.BlockSpec((bsz, d), lambda j: (0, 0)),
                  pl.BlockSpec((d, tn), lambda j: (0, j)),
                  pl.BlockSpec((1, tn), lambda j: (0, j))],
        out_specs=pl.BlockSpec((bsz, tn), lambda j: (0, j)),
        out_shape=jax.ShapeDtypeStruct((bsz, n), F32),
        compiler_params=_params("parallel"),
        name="ada",
    )(c, w_ada, b_ada.reshape(1, n))


def _modulated_norm(x, g, scale, shift):
    ms = jnp.mean(x * x, axis=-1, keepdims=True)
    return x * lax.rsqrt(ms + RMS_EPS) * g * (1.0 + scale) + shift


def _inproj_body(x_ref, g_ref, sc_ref, sh_ref, wuk_ref, wqv_ref, seg_ref, gk_ref, gq_ref,
                 u_ref, k_ref, q_ref, v_ref, *, width, qk_dim, tq):
    x = x_ref[0]
    h = _modulated_norm(x, g_ref[...], sc_ref[0], sh_ref[0]).astype(BF16)
    uk = jnp.dot(h, wuk_ref[...], preferred_element_type=F32)
    u_ref[0] = uk[:, :width].astype(BF16)
    k = uk[:, width:]
    parts = []
    for t in range(width // MXU_DIM):
        kt = k[:, t * MXU_DIM:(t + 1) * MXU_DIM]
        ssq = jnp.dot((kt * kt).astype(BF16), seg_ref[...], preferred_element_type=F32)
        parts.append(kt * lax.rsqrt(ssq * (1.0 / qk_dim) + RMS_EPS))
    k_ref[0] = (jnp.concatenate(parts, axis=1) * gk_ref[...]).astype(BF16)
    qv = lax.dot_general(wqv_ref[...], h, (((1,), (1,)), ((), ())), preferred_element_type=F32)
    tm = qv.shape[1]
    q3 = qv[:width].reshape(width // qk_dim, qk_dim, tm)
    ssq = jnp.sum(q3 * q3, axis=1, keepdims=True)
    qn = (q3 * lax.rsqrt(ssq * (1.0 / qk_dim) + RMS_EPS)).reshape(width, tm) * gq_ref[...]
    vt = qv[width:]
    for r in range(tm // tq):
        q_ref[0, r] = qn[:, r * tq:(r + 1) * tq].astype(BF16)
        v_ref[0, r] = vt[:, r * tq:(r + 1) * tq].astype(BF16)


def _inproj_call(x, g, scale, shift, w_uk, w_qvt, seg, gk_row, gq_col, *, qk_dim, tm, tq):
    bsz, seq, d = x.shape
    width = w_uk.shape[1] // 2
    nq = seq // tq
    r = tm // tq
    tok = lambda b, i: (b, i, 0)
    per_b = lambda b, i: (b, 0, 0)
    body = functools.partial(_inproj_body, width=width, qk_dim=qk_dim, tq=tq)
    return pl.pallas_call(
        body,
        grid=(bsz, seq // tm),
        in_specs=[pl.BlockSpec((1, tm, d), tok),
                  _const_spec((1, d)),
                  pl.BlockSpec((1, 1, d), per_b),
                  pl.BlockSpec((1, 1, d), per_b),
                  _const_spec(w_uk.shape),
                  _const_spec(w_qvt.shape),
                  _const_spec(seg.shape),
                  _const_spec(gk_row.shape),
                  _const_spec(gq_col.shape)],
        out_specs=[pl.BlockSpec((1, tm, width), tok),
                   pl.BlockSpec((1, tm, width), tok),
                   pl.BlockSpec((1, r, width, tq), lambda b, i: (b, i, 0, 0)),
                   pl.BlockSpec((1, r, width, tq), lambda b, i: (b, i, 0, 0))],
        out_shape=[jax.ShapeDtypeStruct((bsz, seq, width), BF16),
                   jax.ShapeDtypeStruct((bsz, seq, width), BF16),
                   jax.ShapeDtypeStruct((bsz, nq, width, tq), BF16),
                   jax.ShapeDtypeStruct((bsz, nq, width, tq), BF16)],
        compiler_params=_params("parallel", "parallel"),
        name="inproj",
    )(x, g, scale, shift, w_uk, w_qvt, seg, gk_row, gq_col)


def _ssm_body(u_ref, perm_ref, unperm_ref, wb_ref, wc_ref, coef_ref, d_ref, wg_ref, bg_ref, o_ref,
              st_ref, hs_ref, y_ref, *, n_tiles):
    bsz, tl, width = u_ref.shape
    rows = bsz * tl
    cols = wb_ref.shape[2] // LANES
    cplx = cols // 2
    shifts = [bsz << i for i in range((SUBLANES // bsz).bit_length() - 1)]

    @pl.when(pl.program_id(0) == 0)
    def _():
        hs_ref[...] = jnp.zeros_like(hs_ref)

    ut = jnp.dot(perm_ref[...], u_ref[...].reshape(rows, width), preferred_element_type=F32)
    ub = ut.astype(BF16)
    for i in range(n_tiles):
        bu = jnp.dot(ub[:, i * LANES:(i + 1) * LANES], wb_ref[i], preferred_element_type=F32)
        for k in range(cols):
            st_ref[i * cols + k] = bu[:, k * LANES:(k + 1) * LANES]

    sub = lax.broadcasted_iota(jnp.int32, (SUBLANES, LANES), 0)

    def cmul_add(xr, xi, cr, ci, sr, si):
        return xr + cr * sr - ci * si, xi + cr * si + ci * sr

    def replicate_last_step(h):
        for sh in reversed(shifts):
            h = jnp.where(sub % (2 * sh) >= sh, h, pltpu.roll(h, sh, 0))
        return h

    def tile_columns(i, _):
        coefs = [[coef_ref[i * cplx + c, n] for n in range(2 * len(shifts) + 2)] for c in range(cplx)]

        def group(g, carry):
            r8 = pl.ds(pl.multiple_of(g * SUBLANES, SUBLANES), SUBLANES)
            out = []
            for c in range(cplx):
                re_col, im_col = i * cols + c, i * cols + cplx + c
                xr, xi = st_ref[re_col, r8, :], st_ref[im_col, r8, :]
                for lvl, sh in enumerate(shifts):
                    xr, xi = cmul_add(xr, xi, coefs[c][2 * lvl], coefs[c][2 * lvl + 1],
                                      pltpu.roll(xr, sh, 0), pltpu.roll(xi, sh, 0))
                xr, xi = cmul_add(xr, xi, coefs[c][-2], coefs[c][-1], *carry[c])
                st_ref[re_col, r8, :] = xr
                st_ref[im_col, r8, :] = xi
                out.append((replicate_last_step(xr), replicate_last_step(xi)))
            return tuple(out)

        init = tuple((hs_ref[i * cplx + c, 0], hs_ref[i * cplx + c, 1]) for c in range(cplx))
        fin = lax.fori_loop(0, rows // SUBLANES, group, init, unroll=2)
        for c in range(cplx):
            hs_ref[i * cplx + c, 0] = fin[c][0]
            hs_ref[i * cplx + c, 1] = fin[c][1]
        return 0

    lax.fori_loop(0, n_tiles, tile_columns, 0)

    for i in range(n_tiles):
        hblk = jnp.concatenate([st_ref[i * cols + k] for k in range(cols)], axis=1).astype(BF16)
        y_ref[:, i * LANES:(i + 1) * LANES] = jnp.dot(hblk, wc_ref[i], preferred_element_type=F32)

    y = y_ref[...] + d_ref[...] * ut
    y = 0.5 * y * (1.0 + lax.erf(y * (2.0 ** -0.5)))
    z = jnp.dot(y.astype(BF16), wg_ref[...], preferred_element_type=F32) + bg_ref[...]
    out = (y * jax.nn.sigmoid(z)).astype(BF16)
    out = jnp.dot(unperm_ref[...], out, preferred_element_type=F32)
    o_ref[...] = out.reshape(bsz, tl, width).astype(BF16)


def _ssm_call(u, wb, wc, coef, d_row, w_glu, b_glu, *, tl):
    bsz, seq, width = u.shape
    n_tiles = wb.shape[0]
    state_cols = n_tiles * wb.shape[2] // LANES
    rows = bsz * tl
    assert SUBLANES % bsz == 0, bsz
    r = jnp.arange(rows)
    perm = (r[None, :] == ((r % bsz) * tl + r // bsz)[:, None]).astype(BF16)
    body = functools.partial(_ssm_body, n_tiles=n_tiles)
    return pl.pallas_call(
        body,
        grid=(seq // tl,),
        in_specs=[pl.BlockSpec((bsz, tl, width), lambda c: (0, c, 0)),
                  _const_spec(perm.shape), _const_spec(perm.shape),
                  _const_spec(wb.shape), _const_spec(wc.shape), _const_spec(coef.shape),
                  _const_spec(d_row.shape), _const_spec(w_glu.shape), _const_spec(b_glu.shape)],
        out_specs=pl.BlockSpec((bsz, tl, width), lambda c: (0, c, 0)),
        out_shape=jax.ShapeDtypeStruct((bsz, seq, width), BF16),
        scratch_shapes=[pltpu.VMEM((state_cols, rows, LANES), F32),
                        pltpu.VMEM((state_cols // 2, 2, SUBLANES, LANES), F32),
                        pltpu.VMEM((rows, width), F32)],
        compiler_params=_params("arbitrary"),
        name="ssm",
    )(u, perm, perm.T, wb, wc, coef, d_row, w_glu, b_glu)


def _ssm_weights(lam_re, lam_im, b_re, b_im, c_re, c_im, log_step, bsz):
    n_groups, n_state = lam_re.shape
    gpt = LANES // b_re.shape[2]
    n_tiles = n_groups // gpt
    lr, li = lam_re.astype(F32), lam_im.astype(F32)
    dt = jnp.exp(log_step.astype(F32))[:, None]
    mag = jnp.exp(dt * lr)
    ar = mag * jnp.cos(dt * li)
    ai = mag * jnp.sin(dt * li)
    den = lr * lr + li * li
    zr = ar - 1.0
    kr = (zr * lr + ai * li) / den
    ki = (ai * lr - zr * li) / den
    br, bi = b_re.astype(F32), b_im.astype(F32)
    bbar_r = kr[..., None] * br - ki[..., None] * bi
    bbar_i = kr[..., None] * bi + ki[..., None] * br
    eye = jnp.eye(gpt, dtype=F32)

    def diag_in(w):
        w = w.reshape(n_tiles, gpt, n_state, -1).transpose(0, 1, 3, 2)
        w = w[:, :, :, None, :] * eye[None, :, None, :, None]
        return w.reshape(n_tiles, gpt * w.shape[2], gpt * n_state)

    def diag_out(w):
        w = w.reshape(n_tiles, gpt, -1, n_state).transpose(0, 1, 3, 2)
        w = w[:, :, :, None, :] * eye[None, :, None, :, None]
        return w.reshape(n_tiles, gpt * n_state, gpt * w.shape[4])

    wb = jnp.concatenate([diag_in(bbar_r), diag_in(bbar_i)], axis=2).astype(BF16)
    wc = jnp.concatenate([diag_out(c_re.astype(F32)), diag_out(-c_im.astype(F32))], axis=1).astype(BF16)
    spv = SUBLANES // bsz
    steps = jnp.arange(1, spv + 1, dtype=F32)[:, None, None]
    pmag = jnp.exp(steps * dt * lr)
    pw_r = (pmag * jnp.cos(steps * dt * li)).reshape(spv, -1, LANES).transpose(1, 0, 2)
    pw_i = (pmag * jnp.sin(steps * dt * li)).reshape(spv, -1, LANES).transpose(1, 0, 2)
    step_of = jnp.arange(SUBLANES)[None, :, None] // bsz
    coef = []
    k = 1
    while k < spv:
        coef += [jnp.where(step_of >= k, pw_r[:, k - 1:k], 0.0), jnp.where(step_of >= k, pw_i[:, k - 1:k], 0.0)]
        k *= 2
    coef += [jnp.repeat(pw_r, bsz, axis=1), jnp.repeat(pw_i, bsz, axis=1)]
    return wb, wc, jnp.stack(coef, axis=1)


def _attn_body(q_ref, k_ref, v_ref, l1_ref, l2_ref, l3_ref, l4_ref, gs_ref, o_ref, acc_ref, sa_ref, sb_ref,
               *, lambda_init, qk_dim):
    qi = pl.program_id(2)
    qt = q_ref[0, 0]
    tq = qt.shape[1]
    tk = tq
    dv = v_ref.shape[2]
    zero = jnp.zeros((qk_dim, tq), qt.dtype)
    qbd = jnp.concatenate([jnp.concatenate([qt[:qk_dim], zero], axis=1),
                           jnp.concatenate([zero, qt[qk_dim:]], axis=1)], axis=0)

    sw = min(tq, MXU_DIM)
    n_str = 2 * tq // sw

    def score_into(j, dst_ref):
        kb = k_ref[0, pl.ds(pl.multiple_of(j * tk, tk), tk), :]
        s = jnp.dot(kb, qbd, preferred_element_type=F32)
        dst_ref[...] = s
        return jnp.max(s, axis=0, keepdims=True)

    def consume(j, src_ref, blk_max, m, masked):
        vb = jnp.concatenate([v_ref[0, j], jnp.ones((SUM_ROWS, tk), BF16)], axis=0)
        ps, alphas, ms = [], [], []
        for c in range(n_str):
            cols = slice(c * sw, (c + 1) * sw)
            q0 = (c * sw) % tq
            nk = min(tk, q0 + sw) if masked else tk
            s_c = src_ref[:nk, cols]
            if masked:
                key_chunk = lax.broadcasted_iota(jnp.int32, s_c.shape, 0) // CHUNK
                qry_chunk = (lax.broadcasted_iota(jnp.int32, s_c.shape, 1) + q0) // CHUNK
                s_c = jnp.where(key_chunk <= qry_chunk, s_c, -jnp.inf)
                blk_max_c = jnp.max(s_c, axis=0, keepdims=True)
            else:
                blk_max_c = blk_max[:, cols]
            m_new = jnp.maximum(m[:, cols], blk_max_c)
            alphas.append(jnp.exp2(m[:, cols] - m_new))
            ms.append(m_new)
            ps.append(jnp.exp2(s_c - m_new).astype(BF16))
        for c in range(n_str):
            cols = slice(c * sw, (c + 1) * sw)
            nk = ps[c].shape[0]
            pv = jnp.dot(vb[:, :nk], ps[c], preferred_element_type=F32)
            acc_ref[:, cols] = alphas[c] * acc_ref[:, cols] + pv
        return jnp.concatenate(ms, axis=1)

    def pair(i, carry):
        m, max_a = carry
        max_b = score_into(2 * i + 1, sb_ref)
        m = consume(2 * i, sa_ref, max_a, m, False)
        max_a = score_into(2 * i + 2, sa_ref)
        m = consume(2 * i + 1, sb_ref, max_b, m, False)
        return m, max_a

    acc_ref[...] = jnp.zeros_like(acc_ref)
    m, max_a = lax.fori_loop(0, qi // 2, pair, (jnp.full((1, 2 * tq), -jnp.inf, F32), score_into(0, sa_ref)))

    @pl.when(qi % 2 == 1)
    def _():
        score_into(qi, sb_ref)
        consume(qi, sb_ref, None, consume(qi - 1, sa_ref, max_a, m, False), True)

    @pl.when(qi % 2 == 0)
    def _():
        consume(qi, sa_ref, None, m, True)

    acc = acc_ref[:dv]
    l = acc_ref[dv:dv + 1]

    lam = (jnp.exp(jnp.sum(l1_ref[...] * l2_ref[...], axis=1, keepdims=True))
           - jnp.exp(jnp.sum(l3_ref[...] * l4_ref[...], axis=1, keepdims=True)) + lambda_init)
    o = acc[:, :tq] / l[:, :tq] - lam * (acc[:, tq:] / l[:, tq:])
    ms = jnp.mean(o * o, axis=0, keepdims=True)
    o = o * lax.rsqrt(ms + RMS_EPS) * gs_ref[...] * (1.0 - lambda_init)
    o_ref[0] = o.T.astype(BF16)


def _attn_call(qt, k, vt, lq1, lk1, lq2, lk2, gs_col, *, lambda_init, qk_dim):
    bsz, nq, width, tq = qt.shape
    seq = k.shape[1]
    dv = gs_col.shape[0]
    heads = width // dv
    body = functools.partial(_attn_body, lambda_init=lambda_init, qk_dim=qk_dim)
    vec = _const_spec(lq1.shape)
    return pl.pallas_call(
        body,
        grid=(bsz, heads, nq),
        in_specs=[pl.BlockSpec((1, 1, dv, tq), lambda b, h, i: (b, i, h, 0)),
                  pl.BlockSpec((1, seq, dv), lambda b, h, i: (b, 0, h)),
                  pl.BlockSpec((1, nq, dv, tq), lambda b, h, i: (b, 0, h, 0)),
                  vec, vec, vec, vec, _const_spec(gs_col.shape)],
        out_specs=pl.BlockSpec((1, tq, dv), lambda b, h, i: (b, i, h)),
        out_shape=jax.ShapeDtypeStruct((bsz, seq, width), BF16),
        scratch_shapes=[pltpu.VMEM((dv + SUM_ROWS, 2 * tq), F32),
                        pltpu.VMEM((tq, 2 * tq), F32),
                        pltpu.VMEM((tq, 2 * tq), F32)],
        compiler_params=_params("parallel", "parallel", "arbitrary"),
        name="attn",
    )(qt, k, vt, lq1, lk1, lq2, lk2, gs_col)


def _outproj_body(x_ref, ys_ref, ya_ref, w_ref, gate_ref, g_ref, sc_ref, sh_ref, x1_ref, h2_ref):
    half = ys_ref.shape[2]
    mixed = (jnp.dot(ys_ref[0], w_ref[:half], preferred_element_type=F32)
             + jnp.dot(ya_ref[0], w_ref[half:], preferred_element_type=F32))
    x1 = x_ref[0] + gate_ref[0] * mixed
    x1_ref[0] = x1
    h2_ref[0] = _modulated_norm(x1, g_ref[...], sc_ref[0], sh_ref[0]).astype(BF16)


def _outproj_call(x, y_ssm, y_att, w_out, gate, g, scale, shift, *, tm):
    bsz, seq, d = x.shape
    half = y_ssm.shape[2]
    tok = lambda b, i: (b, i, 0)
    per_b = lambda b, i: (b, 0, 0)
    return pl.pallas_call(
        _outproj_body,
        grid=(bsz, seq // tm),
        in_specs=[pl.BlockSpec((1, tm, d), tok),
                  pl.BlockSpec((1, tm, half), tok),
                  pl.BlockSpec((1, tm, half), tok),
                  _const_spec(w_out.shape),
                  pl.BlockSpec((1, 1, d), per_b),
                  _const_spec((1, d)),
                  pl.BlockSpec((1, 1, d), per_b),
                  pl.BlockSpec((1, 1, d), per_b)],
        out_specs=[pl.BlockSpec((1, tm, d), tok), pl.BlockSpec((1, tm, d), tok)],
        out_shape=[jax.ShapeDtypeStruct((bsz, seq, d), F32), jax.ShapeDtypeStruct((bsz, seq, d), BF16)],
        compiler_params=_params("parallel", "parallel"),
        name="outproj",
    )(x, y_ssm, y_att, w_out, gate, g, scale, shift)


def _mlp_body(h_ref, x_ref, w1_ref, w2_ref, gate_ref, o_ref):
    kk = pl.program_id(2)

    @pl.when(kk == 0)
    def _():
        o_ref[...] = jnp.zeros_like(o_ref)

    a = jnp.maximum(jnp.dot(h_ref[0], w1_ref[...], preferred_element_type=F32), 0.0)
    o_ref[0] += jnp.dot((a * a).astype(BF16), w2_ref[...], preferred_element_type=F32)

    @pl.when(kk == pl.num_programs(2) - 1)
    def _():
        o_ref[0] = x_ref[0] + gate_ref[0] * o_ref[0]


def _mlp_call(h2, x1, w1, w2, gate, *, tm, th):
    bsz, seq, d = x1.shape
    hidden = w1.shape[1]
    tok = lambda b, i, k: (b, i, 0)
    return pl.pallas_call(
        _mlp_body,
        grid=(bsz, seq // tm, hidden // th),
        in_specs=[pl.BlockSpec((1, tm, d), tok),
                  pl.BlockSpec((1, tm, d), tok),
                  pl.BlockSpec((d, th), lambda b, i, k: (0, k)),
                  pl.BlockSpec((th, d), lambda b, i, k: (k, 0)),
                  pl.BlockSpec((1, 1, d), lambda b, i, k: (b, 0, 0))],
        out_specs=pl.BlockSpec((1, tm, d), tok),
        out_shape=jax.ShapeDtypeStruct((bsz, seq, d), F32),
        compiler_params=_params("parallel", "parallel", "arbitrary"),
        name="mlp",
    )(h2, x1, w1, w2, gate)


def _tile(n, target):
    t = min(n, target)
    assert n % t == 0, (n, target)
    return t


def kernel(x, c, w_ada, b_ada, g_norm_mix, g_norm_mlp, w_in, ssm_lambda_re, ssm_lambda_im, ssm_b_re, ssm_b_im, ssm_c_re, ssm_c_im, ssm_d, ssm_log_step, w_glu, b_glu, g_q, g_k, lambda_q1, lambda_k1, lambda_q2, lambda_k2, g_subln, w_out, w_mlp1, w_mlp2):
    bsz, seq, d = x.shape
    depth = w_ada.shape[0]
    ssm_width = w_glu.shape[1]
    qk_dim = g_q.shape[1]
    dv = g_subln.shape[1]
    attn_width = (w_in.shape[2] - ssm_width) // 3
    n_sub = attn_width // qk_dim
    assert dv == 2 * qk_dim and ssm_width == attn_width

    tm = _tile(seq, 512)
    tq = _tile(seq, 512)
    tl = _tile(seq, 128)
    th = _tile(w_mlp1.shape[2], 1024)

    seg = (jnp.arange(MXU_DIM)[:, None] // qk_dim == jnp.arange(MXU_DIM)[None, :] // qk_dim).astype(BF16)
    row = lambda v: v.reshape(1, -1).astype(F32)

    for l in range(depth):
        lambda_init = 0.8 - 0.6 * math.exp(-0.3 * l)
        mod = _ada_call(c, w_ada[l], b_ada[l])
        shift1, scale1, gate1, shift2, scale2, gate2 = [m.reshape(bsz, 1, d) for m in jnp.split(mod, 6, axis=-1)]

        wl = w_in[l]
        u_w, q_w, k_w, v_w = (wl[:, :ssm_width], wl[:, ssm_width:ssm_width + attn_width],
                              wl[:, ssm_width + attn_width:ssm_width + 2 * attn_width],
                              wl[:, ssm_width + 2 * attn_width:])
        w_uk = jnp.concatenate([u_w, k_w], axis=1).astype(BF16)
        w_qvt = jnp.concatenate([q_w, v_w], axis=1).T.astype(BF16)
        gk_row = jnp.tile(g_k[l].astype(F32), n_sub).reshape(1, attn_width)
        gq_col = (jnp.tile(g_q[l].astype(F32), n_sub) * (qk_dim ** -0.5 * math.log2(math.e))).reshape(attn_width, 1)
        u, k_hat, q_t, v_t = _inproj_call(x, row(g_norm_mix[l]), scale1, shift1, w_uk, w_qvt, seg, gk_row, gq_col,
                                          qk_dim=qk_dim, tm=tm, tq=tq)

        wb, wc, coef = _ssm_weights(ssm_lambda_re[l], ssm_lambda_im[l], ssm_b_re[l], ssm_b_im[l],
                                    ssm_c_re[l], ssm_c_im[l], ssm_log_step[l], bsz)
        y_ssm = _ssm_call(u, wb, wc, coef, row(ssm_d[l]), w_glu[l].astype(BF16), row(b_glu[l]), tl=tl)

        y_att = _attn_call(q_t, k_hat, v_t, row(lambda_q1[l]), row(lambda_k1[l]), row(lambda_q2[l]),
                           row(lambda_k2[l]), g_subln[l].astype(F32).reshape(dv, 1),
                           lambda_init=lambda_init, qk_dim=qk_dim)

        x1, h2 = _outproj_call(x, y_ssm, y_att, w_out[l].astype(BF16), gate1, row(g_norm_mlp[l]), scale2, shift2, tm=tm)
        x = _mlp_call(h2, x1, w_mlp1[l].astype(BF16), w_mlp2[l].astype(BF16), gate2, tm=tm, th=th)
    return x
```

```python
import functools
import math

import jax
import jax.numpy as jnp
from jax import lax
from jax.experimental import pallas as pl
from jax.experimental.pallas import tpu as pltpu

CHUNK = 64
RMS_EPS = 1e-6
LANES = 128
SUBLANES = 8
MXU_DIM = 256
SUM_ROWS = 16
VMEM_LIMIT = 56 * 1024 * 1024

F32 = jnp.float32
BF16 = jnp.bfloat16


def _params(*sem):
    return pltpu.CompilerParams(dimension_semantics=sem, vmem_limit_bytes=VMEM_LIMIT)


def _const_spec(shape):
    nd = len(shape)
    return pl.BlockSpec(shape, lambda *_: (0,) * nd, pipeline_mode=pl.Buffered(1))


def _ada_body(c_ref, w_ref, b_ref, o_ref):
    c = c_ref[...]
    c_act = c * jax.nn.sigmoid(c)
    o_ref[...] = jnp.dot(c_act, w_ref[...], preferred_element_type=F32,
                         precision=lax.Precision.HIGHEST) + b_ref[...]


def _ada_call(c, w_ada, b_ada, tn=2048):
    bsz, d = c.shape
    n = w_ada.shape[1]
    return pl.pallas_call(
        _ada_body,
        grid=(n // tn,),
        in_specs=[pl.BlockSpec((bsz, d), lambda j: (0, 0)),
                  pl.BlockSpec((d, tn), lambda j: (0, j)),
                  pl.BlockSpec((1, tn), lambda j: (0, j))],
        out_specs=pl.BlockSpec((bsz, tn), lambda j: (0, j)),
        out_shape=jax.ShapeDtypeStruct((bsz, n), F32),
        compiler_params=_params("parallel"),
        name="ada",
    )(c, w_ada, b_ada.reshape(1, n))


def _modulated_norm(x, g, scale, shift):
    ms = jnp.mean(x * x, axis=-1, keepdims=True)
    return x * lax.rsqrt(ms + RMS_EPS) * g * (1.0 + scale) + shift


def _inproj_body(x_ref, g_ref, sc_ref, sh_ref, wuk_ref, wqv_ref, seg_ref, gk_ref, gq_ref,
                 u_ref, k_ref, q_ref, v_ref, *, width, qk_dim, tq):
    x = x_ref[0]
    h = _modulated_norm(x, g_ref[...], sc_ref[0], sh_ref[0]).astype(BF16)
    uk = jnp.dot(h, wuk_ref[...], preferred_element_type=F32)
    u_ref[0] = uk[:, :width].astype(BF16)
    k = uk[:, width:]
    parts = []
    for t in range(width // MXU_DIM):
        kt = k[:, t * MXU_DIM:(t + 1) * MXU_DIM]
        ssq = jnp.dot((kt * kt).astype(BF16), seg_ref[...], preferred_element_type=F32)
        parts.append(kt * lax.rsqrt(ssq * (1.0 / qk_dim) + RMS_EPS))
    k_ref[0] = (jnp.concatenate(parts, axis=1) * gk_ref[...]).astype(BF16)
    qv = lax.dot_general(wqv_ref[...], h, (((1,), (1,)), ((), ())), preferred_element_type=F32)
    tm = qv.shape[1]
    q3 = qv[:width].reshape(width // qk_dim, qk_dim, tm)
    ssq = jnp.sum(q3 * q3, axis=1, keepdims=True)
    qn = (q3 * lax.rsqrt(ssq * (1.0 / qk_dim) + RMS_EPS)).reshape(width, tm) * gq_ref[...]
    vt = qv[width:]
    for r in range(tm // tq):
        q_ref[0, r] = qn[:, r * tq:(r + 1) * tq].astype(BF16)
        v_ref[0, r] = vt[:, r * tq:(r + 1) * tq].astype(BF16)


def _inproj_call(x, g, scale, shift, w_uk, w_qvt, seg, gk_row, gq_col, *, qk_dim, tm, tq):
    bsz, seq, d = x.shape
    width = w_uk.shape[1] // 2
    nq = seq // tq
    r = tm // tq
    tok = lambda b, i: (b, i, 0)
    per_b = lambda b, i: (b, 0, 0)
    body = functools.partial(_inproj_body, width=width, qk_dim=qk_dim, tq=tq)
    return pl.pallas_call(
        body,
        grid=(bsz, seq // tm),
        in_specs=[pl.BlockSpec((1, tm, d), tok),
                  _const_spec((1, d)),
                  pl.BlockSpec((1, 1, d), per_b),
                  pl.BlockSpec((1, 1, d), per_b),
                  _const_spec(w_uk.shape),
                  _const_spec(w_qvt.shape),
                  _const_spec(seg.shape),
                  _const_spec(gk_row.shape),
                  _const_spec(gq_col.shape)],
        out_specs=[pl.BlockSpec((1, tm, width), tok),
                   pl.BlockSpec((1, tm, width), tok),
                   pl.BlockSpec((1, r, width, tq), lambda b, i: (b, i, 0, 0)),
                   pl.BlockSpec((1, r, width, tq), lambda b, i: (b, i, 0, 0))],
        out_shape=[jax.ShapeDtypeStruct((bsz, seq, width), BF16),
                   jax.ShapeDtypeStruct((bsz, seq, width), BF16),
                   jax.ShapeDtypeStruct((bsz, nq, width, tq), BF16),
                   jax.ShapeDtypeStruct((bsz, nq, width, tq), BF16)],
        compiler_params=_params("parallel", "parallel"),
        name="inproj",
    )(x, g, scale, shift, w_uk, w_qvt, seg, gk_row, gq_col)


def _ssm_body(u_ref, perm_ref, unperm_ref, wb_ref, wc_ref, coef_ref, d_ref, wg_ref, bg_ref, o_ref,
              st_ref, hs_ref, y_ref, *, n_tiles):
    bsz, tl, width = u_ref.shape
    rows = bsz * tl
    cols = wb_ref.shape[2] // LANES
    cplx = cols // 2
    shifts = [bsz << i for i in range((SUBLANES // bsz).bit_length() - 1)]

    @pl.when(pl.program_id(0) == 0)
    def _():
        hs_ref[...] = jnp.zeros_like(hs_ref)

    ut = jnp.dot(perm_ref[...], u_ref[...].reshape(rows, width), preferred_element_type=F32)
    ub = ut.astype(BF16)
    def project_in(i):
        bu = jnp.dot(ub[:, i * LANES:(i + 1) * LANES], wb_ref[i], preferred_element_type=F32)
        for k in range(cols):
            st_ref[i * cols + k] = bu[:, k * LANES:(k + 1) * LANES]

    def project_out(i):
        hblk = jnp.concatenate([st_ref[i * cols + k] for k in range(cols)], axis=1).astype(BF16)
        y_ref[:, i * LANES:(i + 1) * LANES] = jnp.dot(hblk, wc_ref[i], preferred_element_type=F32)

    sub = lax.broadcasted_iota(jnp.int32, (SUBLANES, LANES), 0)

    def cmul_add(xr, xi, cr, ci, sr, si):
        return xr + cr * sr - ci * si, xi + cr * si + ci * sr

    def replicate_last_step(h):
        for sh in reversed(shifts):
            h = jnp.where(sub % (2 * sh) >= sh, h, pltpu.roll(h, sh, 0))
        return h

    def scan(i):
        coefs = [[coef_ref[i * cplx + c, n] for n in range(2 * len(shifts) + 2)] for c in range(cplx)]
        carry = [(hs_ref[i * cplx + c, 0], hs_ref[i * cplx + c, 1]) for c in range(cplx)]
        for g in range(rows // SUBLANES):
            r8 = slice(g * SUBLANES, (g + 1) * SUBLANES)
            for c in range(cplx):
                re_col, im_col = i * cols + c, i * cols + cplx + c
                xr, xi = st_ref[re_col, r8, :], st_ref[im_col, r8, :]
                for lvl, sh in enumerate(shifts):
                    xr, xi = cmul_add(xr, xi, coefs[c][2 * lvl], coefs[c][2 * lvl + 1],
                                      pltpu.roll(xr, sh, 0), pltpu.roll(xi, sh, 0))
                xr, xi = cmul_add(xr, xi, coefs[c][-2], coefs[c][-1], *carry[c])
                st_ref[re_col, r8, :] = xr
                st_ref[im_col, r8, :] = xi
                carry[c] = (replicate_last_step(xr), replicate_last_step(xi))
        for c in range(cplx):
            hs_ref[i * cplx + c, 0] = carry[c][0]
            hs_ref[i * cplx + c, 1] = carry[c][1]

    project_in(0)
    for i in range(n_tiles):
        if i + 1 < n_tiles:
            project_in(i + 1)
        scan(i)
        project_out(i)

    y = y_ref[...] + d_ref[...] * ut
    y = 0.5 * y * (1.0 + lax.erf(y * (2.0 ** -0.5)))
    z = jnp.dot(y.astype(BF16), wg_ref[...], preferred_element_type=F32) + bg_ref[...]
    out = (y * jax.nn.sigmoid(z)).astype(BF16)
    out = jnp.dot(unperm_ref[...], out, preferred_element_type=F32)
    o_ref[...] = out.reshape(bsz, tl, width).astype(BF16)


def _ssm_call(u, wb, wc, coef, d_row, w_glu, b_glu, *, tl):
    bsz, seq, width = u.shape
    n_tiles = wb.shape[0]
    state_cols = n_tiles * wb.shape[2] // LANES
    rows = bsz * tl
    assert SUBLANES % bsz == 0, bsz
    r = jnp.arange(rows)
    perm = (r[None, :] == ((r % bsz) * tl + r // bsz)[:, None]).astype(BF16)
    body = functools.partial(_ssm_body, n_tiles=n_tiles)
    return pl.pallas_call(
        body,
        grid=(seq // tl,),
        in_specs=[pl.BlockSpec((bsz, tl, width), lambda c: (0, c, 0)),
                  _const_spec(perm.shape), _const_spec(perm.shape),
                  _const_spec(wb.shape), _const_spec(wc.shape), _const_spec(coef.shape),
                  _const_spec(d_row.shape), _const_spec(w_glu.shape), _const_spec(b_glu.shape)],
        out_specs=pl.BlockSpec((bsz, tl, width), lambda c: (0, c, 0)),
        out_shape=jax.ShapeDtypeStruct((bsz, seq, width), BF16),
        scratch_shapes=[pltpu.VMEM((state_cols, rows, LANES), F32),
                        pltpu.VMEM((state_cols // 2, 2, SUBLANES, LANES), F32),
                        pltpu.VMEM((rows, width), F32)],
        compiler_params=_params("arbitrary"),
        name="ssm",
    )(u, perm, perm.T, wb, wc, coef, d_row, w_glu, b_glu)


def _ssm_weights(lam_re, lam_im, b_re, b_im, c_re, c_im, log_step, bsz):
    n_groups, n_state = lam_re.shape
    gpt = LANES // b_re.shape[2]
    n_tiles = n_groups // gpt
    lr, li = lam_re.astype(F32), lam_im.astype(F32)
    dt = jnp.exp(log_step.astype(F32))[:, None]
    mag = jnp.exp(dt * lr)
    ar = mag * jnp.cos(dt * li)
    ai = mag * jnp.sin(dt * li)
    den = lr * lr + li * li
    zr = ar - 1.0
    kr = (zr * lr + ai * li) / den
    ki = (ai * lr - zr * li) / den
    br, bi = b_re.astype(F32), b_im.astype(F32)
    bbar_r = kr[..., None] * br - ki[..., None] * bi
    bbar_i = kr[..., None] * bi + ki[..., None] * br
    eye = jnp.eye(gpt, dtype=F32)

    def diag_in(w):
        w = w.reshape(n_tiles, gpt, n_state, -1).transpose(0, 1, 3, 2)
        w = w[:, :, :, None, :] * eye[None, :, None, :, None]
        return w.reshape(n_tiles, gpt * w.shape[2], gpt * n_state)

    def diag_out(w):
        w = w.reshape(n_tiles, gpt, -1, n_state).transpose(0, 1, 3, 2)
        w = w[:, :, :, None, :] * eye[None, :, None, :, None]
        return w.reshape(n_tiles, gpt * n_state, gpt * w.shape[4])

    wb = jnp.concatenate([diag_in(bbar_r), diag_in(bbar_i)], axis=2).astype(BF16)
    wc = jnp.concatenate([diag_out(c_re.astype(F32)), diag_out(-c_im.astype(F32))], axis=1).astype(BF16)
    spv = SUBLANES // bsz
    steps = jnp.arange(1, spv + 1, dtype=F32)[:, None, None]
    pmag = jnp.exp(steps * dt * lr)
    pw_r = (pmag * jnp.cos(steps * dt * li)).reshape(spv, -1, LANES).transpose(1, 0, 2)
    pw_i = (pmag * jnp.sin(steps * dt * li)).reshape(spv, -1, LANES).transpose(1, 0, 2)
    step_of = jnp.arange(SUBLANES)[None, :, None] // bsz
    coef = []
    k = 1
    while k < spv:
        coef += [jnp.where(step_of >= k, pw_r[:, k - 1:k], 0.0), jnp.where(step_of >= k, pw_i[:, k - 1:k], 0.0)]
        k *= 2
    coef += [jnp.repeat(pw_r, bsz, axis=1), jnp.repeat(pw_i, bsz, axis=1)]
    return wb, wc, jnp.stack(coef, axis=1)


def _attn_body(q_ref, k_ref, v_ref, l1_ref, l2_ref, l3_ref, l4_ref, gs_ref, o_ref, acc_ref, *score_refs,
               lambda_init, qk_dim, dv):
    qi = pl.program_id(2)
    tk = q_ref.shape[3]
    assert q_ref.shape[1] == 2
    tq = 2 * tk
    heads = q_ref.shape[2] // dv
    sa_refs, sb_refs = score_refs[:heads], score_refs[heads:]
    zero = jnp.zeros((qk_dim, tq), q_ref.dtype)
    qbds = []
    for h in range(heads):
        rows_h = slice(h * dv, (h + 1) * dv)
        qt = jnp.concatenate([q_ref[0, 0, rows_h, :], q_ref[0, 1, rows_h, :]], axis=1)
        qbds.append(jnp.concatenate([jnp.concatenate([qt[:qk_dim], zero], axis=1),
                                     jnp.concatenate([zero, qt[qk_dim:]], axis=1)], axis=0))

    sw = min(tq, MXU_DIM)
    n_str = 2 * tq // sw

    def score_into(h, j, dst_ref):
        kb = k_ref[0, pl.ds(pl.multiple_of(j * tk, tk), tk), h * dv:(h + 1) * dv]
        s = jnp.dot(kb, qbds[h], preferred_element_type=F32)
        dst_ref[...] = s
        return jnp.max(s, axis=0, keepdims=True)

    def consume(h, j, src_ref, blk_max, m, key_off=None):
        vb = jnp.concatenate([v_ref[0, j, h * dv:(h + 1) * dv, :], jnp.ones((SUM_ROWS, tk), BF16)], axis=0)
        ps, alphas, ms = [], [], []
        for c in range(n_str):
            cols = slice(c * sw, (c + 1) * sw)
            q0 = (c * sw) % tq
            nk = tk if key_off is None else max(0, min(tk, q0 + sw - key_off))
            if nk == 0:
                ps.append(None)
                alphas.append(None)
                ms.append(m[:, cols])
                continue
            s_c = src_ref[:nk, cols]
            if key_off is None:
                blk_max_c = blk_max[:, cols]
            else:
                if q0 < key_off + nk:
                    key_chunk = (lax.broadcasted_iota(jnp.int32, s_c.shape, 0) + key_off) // CHUNK
                    qry_chunk = (lax.broadcasted_iota(jnp.int32, s_c.shape, 1) + q0) // CHUNK
                    s_c = jnp.where(key_chunk <= qry_chunk, s_c, -jnp.inf)
                blk_max_c = jnp.max(s_c, axis=0, keepdims=True)
            m_new = jnp.maximum(m[:, cols], blk_max_c)
            alphas.append(jnp.exp2(m[:, cols] - m_new))
            ms.append(m_new)
            ps.append(jnp.exp2(s_c - m_new).astype(BF16))
        for c in range(n_str):
            if ps[c] is None:
                continue
            cols = slice(c * sw, (c + 1) * sw)
            nk = ps[c].shape[0]
            pv = jnp.dot(vb[:, :nk], ps[c], preferred_element_type=F32)
            acc_ref[h, :, cols] = alphas[c] * acc_ref[h, :, cols] + pv
        return jnp.concatenate(ms, axis=1)

    def step(j_next, dst_refs, j, src_refs, maxes, ms, key_off=None):
        new_max = [score_into(h, j_next, dst_refs[h]) for h in range(heads)] if j_next is not None else None
        ms = [consume(h, j, src_refs[h], maxes[h], ms[h], key_off) for h in range(heads)]
        return new_max, ms

    def pair(i, carry):
        ms, max_a = carry
        max_b, ms = step(2 * i + 1, sb_refs, 2 * i, sa_refs, max_a, ms)
        max_a, ms = step(2 * i + 2, sa_refs, 2 * i + 1, sb_refs, max_b, ms)
        return ms, max_a

    acc_ref[...] = jnp.zeros_like(acc_ref)
    init = ([jnp.full((1, 2 * tq), -jnp.inf, F32)] * heads, [score_into(h, 0, sa_refs[h]) for h in range(heads)])
    ms, _ = lax.fori_loop(0, qi, pair, init)
    no_max = [None] * heads
    _, ms = step(2 * qi + 1, sb_refs, 2 * qi, sa_refs, no_max, ms, key_off=0)
    step(None, None, 2 * qi + 1, sb_refs, no_max, ms, key_off=tk)

    lam = (jnp.exp(jnp.sum(l1_ref[...] * l2_ref[...], axis=1, keepdims=True))
           - jnp.exp(jnp.sum(l3_ref[...] * l4_ref[...], axis=1, keepdims=True)) + lambda_init)
    for h in range(heads):
        acc = acc_ref[h, :dv]
        l = acc_ref[h, dv:dv + 1]
        o = acc[:, :tq] / l[:, :tq] - lam * (acc[:, tq:] / l[:, tq:])
        ms_o = jnp.mean(o * o, axis=0, keepdims=True)
        o = o * lax.rsqrt(ms_o + RMS_EPS) * gs_ref[...] * (1.0 - lambda_init)
        o_ref[0, :, h * dv:(h + 1) * dv] = o.T.astype(BF16)


def _attn_call(qt, k, vt, lq1, lk1, lq2, lk2, gs_col, *, lambda_init, qk_dim):
    bsz, nk, width, tk = qt.shape
    seq = k.shape[1]
    dv = gs_col.shape[0]
    heads = width // dv
    hps = 2 if heads % 2 == 0 else 1
    hw = hps * dv
    tq = 2 * tk
    assert nk % 2 == 0 and tk % CHUNK == 0, (nk, tk)
    body = functools.partial(_attn_body, lambda_init=lambda_init, qk_dim=qk_dim, dv=dv)
    vec = _const_spec(lq1.shape)
    return pl.pallas_call(
        body,
        grid=(bsz, heads // hps, nk // 2),
        in_specs=[pl.BlockSpec((1, 2, hw, tk), lambda b, h, i: (b, i, h, 0)),
                  pl.BlockSpec((1, seq, hw), lambda b, h, i: (b, 0, h)),
                  pl.BlockSpec((1, nk, hw, tk), lambda b, h, i: (b, 0, h, 0)),
                  vec, vec, vec, vec, _const_spec(gs_col.shape)],
        out_specs=pl.BlockSpec((1, tq, hw), lambda b, h, i: (b, i, h)),
        out_shape=jax.ShapeDtypeStruct((bsz, seq, width), BF16),
        scratch_shapes=([pltpu.VMEM((hps, dv + SUM_ROWS, 2 * tq), F32)]
                        + [pltpu.VMEM((tk, 2 * tq), F32)] * (2 * hps)),
        compiler_params=_params("parallel", "parallel", "arbitrary"),
        name="attn",
    )(qt, k, vt, lq1, lk1, lq2, lk2, gs_col)


def _outproj_body(x_ref, ys_ref, ya_ref, w_ref, gate_ref, g_ref, sc_ref, sh_ref, x1_ref, h2_ref):
    half = ys_ref.shape[2]
    mixed = (jnp.dot(ys_ref[0], w_ref[:half], preferred_element_type=F32)
             + jnp.dot(ya_ref[0], w_ref[half:], preferred_element_type=F32))
    x1 = x_ref[0] + gate_ref[0] * mixed
    x1_ref[0] = x1
    h2_ref[0] = _modulated_norm(x1, g_ref[...], sc_ref[0], sh_ref[0]).astype(BF16)


def _outproj_call(x, y_ssm, y_att, w_out, gate, g, scale, shift, *, tm):
    bsz, seq, d = x.shape
    half = y_ssm.shape[2]
    tok = lambda b, i: (b, i, 0)
    per_b = lambda b, i: (b, 0, 0)
    return pl.pallas_call(
        _outproj_body,
        grid=(bsz, seq // tm),
        in_specs=[pl.BlockSpec((1, tm, d), tok),
                  pl.BlockSpec((1, tm, half), tok),
                  pl.BlockSpec((1, tm, half), tok),
                  _const_spec(w_out.shape),
                  pl.BlockSpec((1, 1, d), per_b),
                  _const_spec((1, d)),
                  pl.BlockSpec((1, 1, d), per_b),
                  pl.BlockSpec((1, 1, d), per_b)],
        out_specs=[pl.BlockSpec((1, tm, d), tok), pl.BlockSpec((1, tm, d), tok)],
        out_shape=[jax.ShapeDtypeStruct((bsz, seq, d), F32), jax.ShapeDtypeStruct((bsz, seq, d), BF16)],
        compiler_params=_params("parallel", "parallel"),
        name="outproj",
    )(x, y_ssm, y_att, w_out, gate, g, scale, shift)


def _mlp_body(h_ref, x_ref, w1_ref, w2_ref, gate_ref, o_ref):
    kk = pl.program_id(2)

    @pl.when(kk == 0)
    def _():
        o_ref[...] = jnp.zeros_like(o_ref)

    a = jnp.maximum(jnp.dot(h_ref[0], w1_ref[...], preferred_element_type=F32), 0.0)
    o_ref[0] += jnp.dot((a * a).astype(BF16), w2_ref[...], preferred_element_type=F32)

    @pl.when(kk == pl.num_programs(2) - 1)
    def _():
        o_ref[0] = x_ref[0] + gate_ref[0] * o_ref[0]


def _mlp_call(h2, x1, w1, w2, gate, *, tm, th):
    bsz, seq, d = x1.shape
    hidden = w1.shape[1]
    tok = lambda b, i, k: (b, i, 0)
    return pl.pallas_call(
        _mlp_body,
        grid=(bsz, seq // tm, hidden // th),
        in_specs=[pl.BlockSpec((1, tm, d), tok),
                  pl.BlockSpec((1, tm, d), tok),
                  pl.BlockSpec((d, th), lambda b, i, k: (0, k)),
                  pl.BlockSpec((th, d), lambda b, i, k: (k, 0)),
                  pl.BlockSpec((1, 1, d), lambda b, i, k: (b, 0, 0))],
        out_specs=pl.BlockSpec((1, tm, d), tok),
        out_shape=jax.ShapeDtypeStruct((bsz, seq, d), F32),
        compiler_params=_params("parallel", "parallel", "arbitrary"),
        name="mlp",
    )(h2, x1, w1, w2, gate)


def _tile(n, target):
    t = min(n, target)
    assert n % t == 0, (n, target)
    return t


def kernel(x, c, w_ada, b_ada, g_norm_mix, g_norm_mlp, w_in, ssm_lambda_re, ssm_lambda_im, ssm_b_re, ssm_b_im, ssm_c_re, ssm_c_im, ssm_d, ssm_log_step, w_glu, b_glu, g_q, g_k, lambda_q1, lambda_k1, lambda_q2, lambda_k2, g_subln, w_out, w_mlp1, w_mlp2):
    bsz, seq, d = x.shape
    depth = w_ada.shape[0]
    ssm_width = w_glu.shape[1]
    qk_dim = g_q.shape[1]
    dv = g_subln.shape[1]
    attn_width = (w_in.shape[2] - ssm_width) // 3
    n_sub = attn_width // qk_dim
    assert dv == 2 * qk_dim and ssm_width == attn_width

    tm = _tile(seq, 512)
    tq = _tile(seq, 512)
    tl = _tile(seq, 128)
    th = _tile(w_mlp1.shape[2], 1024)

    seg = (jnp.arange(MXU_DIM)[:, None] // qk_dim == jnp.arange(MXU_DIM)[None, :] // qk_dim).astype(BF16)
    row = lambda v: v.reshape(1, -1).astype(F32)

    for l in range(depth):
        lambda_init = 0.8 - 0.6 * math.exp(-0.3 * l)
        mod = _ada_call(c, w_ada[l], b_ada[l])
        shift1, scale1, gate1, shift2, scale2, gate2 = [m.reshape(bsz, 1, d) for m in jnp.split(mod, 6, axis=-1)]

        wl = w_in[l]
        u_w, q_w, k_w, v_w = (wl[:, :ssm_width], wl[:, ssm_width:ssm_width + attn_width],
                              wl[:, ssm_width + attn_width:ssm_width + 2 * attn_width],
                              wl[:, ssm_width + 2 * attn_width:])
        w_uk = jnp.concatenate([u_w, k_w], axis=1).astype(BF16)
        w_qvt = jnp.concatenate([q_w, v_w], axis=1).T.astype(BF16)
        gk_row = jnp.tile(g_k[l].astype(F32), n_sub).reshape(1, attn_width)
        gq_col = (jnp.tile(g_q[l].astype(F32), n_sub) * (qk_dim ** -0.5 * math.log2(math.e))).reshape(attn_width, 1)
        u, k_hat, q_t, v_t = _inproj_call(x, row(g_norm_mix[l]), scale1, shift1, w_uk, w_qvt, seg, gk_row, gq_col,
                                          qk_dim=qk_dim, tm=tm, tq=tq)

        wb, wc, coef = _ssm_weights(ssm_lambda_re[l], ssm_lambda_im[l], ssm_b_re[l], ssm_b_im[l],
                                    ssm_c_re[l], ssm_c_im[l], ssm_log_step[l], bsz)
        y_ssm = _ssm_call(u, wb, wc, coef, row(ssm_d[l]), w_glu[l].astype(BF16), row(b_glu[l]), tl=tl)

        y_att = _attn_call(q_t, k_hat, v_t, row(lambda_q1[l]), row(lambda_k1[l]), row(lambda_q2[l]),
                           row(lambda_k2[l]), g_subln[l].astype(F32).reshape(dv, 1),
                           lambda_init=lambda_init, qk_dim=qk_dim)

        x1, h2 = _outproj_call(x, y_ssm, y_att, w_out[l].astype(BF16), gate1, row(g_norm_mlp[l]), scale2, shift2, tm=tm)
        x = _mlp_call(h2, x1, w_mlp1[l].astype(BF16), w_mlp2[l].astype(BF16), gate2, tm=tm, th=th)
    return x
```

```python
import functools
import math

import jax
import jax.numpy as jnp
from jax import lax
from jax.experimental import pallas as pl
from jax.experimental.pallas import tpu as pltpu

CHUNK = 64
RMS_EPS = 1e-6
LANES = 128
SUBLANES = 8
MXU_DIM = 256
SUM_ROWS = 16
VMEM_LIMIT = 60 * 1024 * 1024

F32 = jnp.float32
BF16 = jnp.bfloat16


def _params(*sem):
    return pltpu.CompilerParams(dimension_semantics=sem, vmem_limit_bytes=VMEM_LIMIT)


def _const_spec(shape):
    nd = len(shape)
    return pl.BlockSpec(shape, lambda *_: (0,) * nd, pipeline_mode=pl.Buffered(1))


def _ada_body(c_ref, w_ref, b_ref, o_ref):
    c = c_ref[...]
    c_act = c * jax.nn.sigmoid(c)
    o_ref[...] = jnp.dot(c_act, w_ref[...], preferred_element_type=F32,
                         precision=lax.Precision.HIGHEST) + b_ref[...]


def _ada_call(c, w_ada, b_ada, tn=1024):
    bsz, d = c.shape
    n = w_ada.shape[1]
    return pl.pallas_call(
        _ada_body,
        grid=(n // tn,),
        in_specs=[pl.BlockSpec((bsz, d), lambda j: (0, 0)),
                  pl.BlockSpec((d, tn), lambda j: (0, j)),
                  pl.BlockSpec((1, tn), lambda j: (0, j))],
        out_specs=pl.BlockSpec((bsz, tn), lambda j: (0, j)),
        out_shape=jax.ShapeDtypeStruct((bsz, n), F32),
        compiler_params=_params("parallel"),
        name="ada",
    )(c, w_ada, b_ada.reshape(1, n))


def _modulated_norm(x, g, scale, shift):
    ms = jnp.mean(x * x, axis=-1, keepdims=True)
    return x * lax.rsqrt(ms + RMS_EPS) * g * (1.0 + scale) + shift


def _inproj_body(x_ref, g_ref, sc_ref, sh_ref, wuk_ref, wqv_ref, seg_ref, gk_ref, gq_ref,
                 u_ref, k_ref, q_ref, v_ref, *, width, qk_dim, tq):
    x = x_ref[0]
    h = _modulated_norm(x, g_ref[...], sc_ref[0], sh_ref[0]).astype(BF16)
    uk = jnp.dot(h, wuk_ref[...], preferred_element_type=F32)
    u_ref[0] = uk[:, :width].astype(BF16)
    k = uk[:, width:]
    parts = []
    for t in range(width // MXU_DIM):
        kt = k[:, t * MXU_DIM:(t + 1) * MXU_DIM]
        ssq = jnp.dot((kt * kt).astype(BF16), seg_ref[...], preferred_element_type=F32)
        parts.append(kt * lax.rsqrt(ssq * (1.0 / qk_dim) + RMS_EPS))
    k_ref[0] = (jnp.concatenate(parts, axis=1) * gk_ref[...]).astype(BF16)
    qv = lax.dot_general(wqv_ref[...], h, (((1,), (1,)), ((), ())), preferred_element_type=F32)
    tm = qv.shape[1]
    q3 = qv[:width].reshape(width // qk_dim, qk_dim, tm)
    ssq = jnp.sum(q3 * q3, axis=1, keepdims=True)
    qn = (q3 * lax.rsqrt(ssq * (1.0 / qk_dim) + RMS_EPS)).reshape(width, tm) * gq_ref[...]
    vt = qv[width:]
    for r in range(tm // tq):
        q_ref[0, r] = qn[:, r * tq:(r + 1) * tq].astype(BF16)
        v_ref[0, r] = vt[:, r * tq:(r + 1) * tq].astype(BF16)


def _inproj_call(x, g, scale, shift, w_uk, w_qvt, seg, gk_row, gq_col, *, qk_dim, tm, tq):
    bsz, seq, d = x.shape
    width = w_uk.shape[1] // 2
    nq = seq // tq
    r = tm // tq
    tok = lambda b, i: (b, i, 0)
    per_b = lambda b, i: (b, 0, 0)
    body = functools.partial(_inproj_body, width=width, qk_dim=qk_dim, tq=tq)
    return pl.pallas_call(
        body,
        grid=(bsz, seq // tm),
        in_specs=[pl.BlockSpec((1, tm, d), tok),
                  _const_spec((1, d)),
                  pl.BlockSpec((1, 1, d), per_b),
                  pl.BlockSpec((1, 1, d), per_b),
                  _const_spec(w_uk.shape),
                  _const_spec(w_qvt.shape),
                  _const_spec(seg.shape),
                  _const_spec(gk_row.shape),
                  _const_spec(gq_col.shape)],
        out_specs=[pl.BlockSpec((1, tm, width), tok),
                   pl.BlockSpec((1, tm, width), tok),
                   pl.BlockSpec((1, r, width, tq), lambda b, i: (b, i, 0, 0)),
                   pl.BlockSpec((1, r, width, tq), lambda b, i: (b, i, 0, 0))],
        out_shape=[jax.ShapeDtypeStruct((bsz, seq, width), BF16),
                   jax.ShapeDtypeStruct((bsz, seq, width), BF16),
                   jax.ShapeDtypeStruct((bsz, nq, width, tq), BF16),
                   jax.ShapeDtypeStruct((bsz, nq, width, tq), BF16)],
        compiler_params=_params("parallel", "parallel"),
        name="inproj",
    )(x, g, scale, shift, w_uk, w_qvt, seg, gk_row, gq_col)


def _ssm_body(u_ref, perm_ref, unperm_ref, wb_ref, wc_ref, coef_ref, d_ref, wg_ref, bg_ref, o_ref,
              st_ref, hs_ref, y_ref, *, n_tiles):
    bsz, tl, width = u_ref.shape
    rows = bsz * tl
    cols = wb_ref.shape[2] // LANES
    cplx = cols // 2
    shifts = [bsz << i for i in range((SUBLANES // bsz).bit_length() - 1)]

    @pl.when(pl.program_id(0) == 0)
    def _():
        hs_ref[...] = jnp.zeros_like(hs_ref)

    ut = jnp.dot(perm_ref[...], u_ref[...].reshape(rows, width), preferred_element_type=F32)
    ub = ut.astype(BF16)
    def project_in(i):
        bu = jnp.dot(ub[:, i * LANES:(i + 1) * LANES], wb_ref[i], preferred_element_type=F32)
        for k in range(cols):
            st_ref[i * cols + k] = bu[:, k * LANES:(k + 1) * LANES]

    def project_out(i):
        hblk = jnp.concatenate([st_ref[i * cols + k] for k in range(cols)], axis=1).astype(BF16)
        y_ref[:, i * LANES:(i + 1) * LANES] = jnp.dot(hblk, wc_ref[i], preferred_element_type=F32)

    sub = lax.broadcasted_iota(jnp.int32, (SUBLANES, LANES), 0)

    def cmul_add(xr, xi, cr, ci, sr, si):
        return xr + cr * sr - ci * si, xi + cr * si + ci * sr

    def replicate_last_step(h):
        for sh in reversed(shifts):
            h = jnp.where(sub % (2 * sh) >= sh, h, pltpu.roll(h, sh, 0))
        return h

    def scan(i):
        coefs = [[coef_ref[i * cplx + c, n] for n in range(2 * len(shifts) + 2)] for c in range(cplx)]
        carry = [(hs_ref[i * cplx + c, 0], hs_ref[i * cplx + c, 1]) for c in range(cplx)]
        for g in range(rows // SUBLANES):
            r8 = slice(g * SUBLANES, (g + 1) * SUBLANES)
            for c in range(cplx):
                re_col, im_col = i * cols + c, i * cols + cplx + c
                xr, xi = st_ref[re_col, r8, :], st_ref[im_col, r8, :]
                for lvl, sh in enumerate(shifts):
                    xr, xi = cmul_add(xr, xi, coefs[c][2 * lvl], coefs[c][2 * lvl + 1],
                                      pltpu.roll(xr, sh, 0), pltpu.roll(xi, sh, 0))
                xr, xi = cmul_add(xr, xi, coefs[c][-2], coefs[c][-1], *carry[c])
                st_ref[re_col, r8, :] = xr
                st_ref[im_col, r8, :] = xi
                carry[c] = (replicate_last_step(xr), replicate_last_step(xi))
        for c in range(cplx):
            hs_ref[i * cplx + c, 0] = carry[c][0]
            hs_ref[i * cplx + c, 1] = carry[c][1]

    project_in(0)
    for i in range(n_tiles):
        if i + 1 < n_tiles:
            project_in(i + 1)
        scan(i)
        project_out(i)

    y = y_ref[...] + d_ref[...] * ut
    y = 0.5 * y * (1.0 + lax.erf(y * (2.0 ** -0.5)))
    z = jnp.dot(y.astype(BF16), wg_ref[...], preferred_element_type=F32) + bg_ref[...]
    out = (y * jax.nn.sigmoid(z)).astype(BF16)
    out = jnp.dot(unperm_ref[...], out, preferred_element_type=F32)
    o_ref[...] = out.reshape(bsz, tl, width).astype(BF16)


def _ssm_call(u, wb, wc, coef, d_row, w_glu, b_glu, *, tl):
    bsz, seq, width = u.shape
    n_tiles = wb.shape[0]
    state_cols = n_tiles * wb.shape[2] // LANES
    rows = bsz * tl
    assert SUBLANES % bsz == 0, bsz
    r = jnp.arange(rows)
    perm = (r[None, :] == ((r % bsz) * tl + r // bsz)[:, None]).astype(BF16)
    body = functools.partial(_ssm_body, n_tiles=n_tiles)
    return pl.pallas_call(
        body,
        grid=(seq // tl,),
        in_specs=[pl.BlockSpec((bsz, tl, width), lambda c: (0, c, 0)),
                  _const_spec(perm.shape), _const_spec(perm.shape),
                  _const_spec(wb.shape), _const_spec(wc.shape), _const_spec(coef.shape),
                  _const_spec(d_row.shape), _const_spec(w_glu.shape), _const_spec(b_glu.shape)],
        out_specs=pl.BlockSpec((bsz, tl, width), lambda c: (0, c, 0)),
        out_shape=jax.ShapeDtypeStruct((bsz, seq, width), BF16),
        scratch_shapes=[pltpu.VMEM((state_cols, rows, LANES), F32),
                        pltpu.VMEM((state_cols // 2, 2, SUBLANES, LANES), F32),
                        pltpu.VMEM((rows, width), F32)],
        compiler_params=_params("arbitrary"),
        name="ssm",
    )(u, perm, perm.T, wb, wc, coef, d_row, w_glu, b_glu)


def _ssm_weights(lam_re, lam_im, b_re, b_im, c_re, c_im, log_step, bsz):
    n_groups, n_state = lam_re.shape
    gpt = LANES // b_re.shape[2]
    n_tiles = n_groups // gpt
    lr, li = lam_re.astype(F32), lam_im.astype(F32)
    dt = jnp.exp(log_step.astype(F32))[:, None]
    mag = jnp.exp(dt * lr)
    ar = mag * jnp.cos(dt * li)
    ai = mag * jnp.sin(dt * li)
    den = lr * lr + li * li
    zr = ar - 1.0
    kr = (zr * lr + ai * li) / den
    ki = (ai * lr - zr * li) / den
    br, bi = b_re.astype(F32), b_im.astype(F32)
    bbar_r = kr[..., None] * br - ki[..., None] * bi
    bbar_i = kr[..., None] * bi + ki[..., None] * br
    eye = jnp.eye(gpt, dtype=F32)

    def diag_in(w):
        w = w.reshape(n_tiles, gpt, n_state, -1).transpose(0, 1, 3, 2)
        w = w[:, :, :, None, :] * eye[None, :, None, :, None]
        return w.reshape(n_tiles, gpt * w.shape[2], gpt * n_state)

    def diag_out(w):
        w = w.reshape(n_tiles, gpt, -1, n_state).transpose(0, 1, 3, 2)
        w = w[:, :, :, None, :] * eye[None, :, None, :, None]
        return w.reshape(n_tiles, gpt * n_state, gpt * w.shape[4])

    wb = jnp.concatenate([diag_in(bbar_r), diag_in(bbar_i)], axis=2).astype(BF16)
    wc = jnp.concatenate([diag_out(c_re.astype(F32)), diag_out(-c_im.astype(F32))], axis=1).astype(BF16)
    spv = SUBLANES // bsz
    steps = jnp.arange(1, spv + 1, dtype=F32)[:, None, None]
    pmag = jnp.exp(steps * dt * lr)
    pw_r = (pmag * jnp.cos(steps * dt * li)).reshape(spv, -1, LANES).transpose(1, 0, 2)
    pw_i = (pmag * jnp.sin(steps * dt * li)).reshape(spv, -1, LANES).transpose(1, 0, 2)
    step_of = jnp.arange(SUBLANES)[None, :, None] // bsz
    coef = []
    k = 1
    while k < spv:
        coef += [jnp.where(step_of >= k, pw_r[:, k - 1:k], 0.0), jnp.where(step_of >= k, pw_i[:, k - 1:k], 0.0)]
        k *= 2
    coef += [jnp.repeat(pw_r, bsz, axis=1), jnp.repeat(pw_i, bsz, axis=1)]
    return wb, wc, jnp.stack(coef, axis=1)


def _attn_body(q_ref, k_ref, v_ref, l1_ref, l2_ref, l3_ref, l4_ref, gs_ref, o_ref, acc_ref, *score_refs,
               lambda_init, qk_dim, dv):
    qi = pl.program_id(2)
    tk = q_ref.shape[3]
    assert q_ref.shape[1] == 2
    tq = 2 * tk
    heads = q_ref.shape[2] // dv
    sa_refs, sb_refs = score_refs[:heads], score_refs[heads:]
    zero = jnp.zeros((qk_dim, tq), q_ref.dtype)
    qbds = []
    for h in range(heads):
        rows_h = slice(h * dv, (h + 1) * dv)
        qt = jnp.concatenate([q_ref[0, 0, rows_h, :], q_ref[0, 1, rows_h, :]], axis=1)
        qbds.append(jnp.concatenate([jnp.concatenate([qt[:qk_dim], zero], axis=1),
                                     jnp.concatenate([zero, qt[qk_dim:]], axis=1)], axis=0))

    sw = min(tq, MXU_DIM)
    n_str = 2 * tq // sw

    def score_into(h, j, dst_ref):
        kb = k_ref[0, pl.ds(pl.multiple_of(j * tk, tk), tk), h * dv:(h + 1) * dv]
        s = jnp.dot(kb, qbds[h], preferred_element_type=F32)
        dst_ref[...] = s
        return jnp.max(s, axis=0, keepdims=True)

    def consume(h, j, src_ref, blk_max, m, key_off=None):
        vb = jnp.concatenate([v_ref[0, j, h * dv:(h + 1) * dv, :], jnp.ones((SUM_ROWS, tk), BF16)], axis=0)
        ps, alphas, ms = [], [], []
        for c in range(n_str):
            cols = slice(c * sw, (c + 1) * sw)
            q0 = (c * sw) % tq
            nk = tk if key_off is None else max(0, min(tk, q0 + sw - key_off))
            if nk == 0:
                ps.append(None)
                alphas.append(None)
                ms.append(m[:, cols])
                continue
            s_c = src_ref[:nk, cols]
            if key_off is None:
                blk_max_c = blk_max[:, cols]
            else:
                if q0 < key_off + nk:
                    key_chunk = (lax.broadcasted_iota(jnp.int32, s_c.shape, 0) + key_off) // CHUNK
                    qry_chunk = (lax.broadcasted_iota(jnp.int32, s_c.shape, 1) + q0) // CHUNK
                    s_c = jnp.where(key_chunk <= qry_chunk, s_c, -jnp.inf)
                blk_max_c = jnp.max(s_c, axis=0, keepdims=True)
            m_new = jnp.maximum(m[:, cols], blk_max_c)
            alphas.append(jnp.exp2(m[:, cols] - m_new))
            ms.append(m_new)
            ps.append(jnp.exp2(s_c - m_new).astype(BF16))
        for c in range(n_str):
            if ps[c] is None:
                continue
            cols = slice(c * sw, (c + 1) * sw)
            nk = ps[c].shape[0]
            pv = jnp.dot(vb[:, :nk], ps[c], preferred_element_type=F32)
            acc_ref[h, :, cols] = alphas[c] * acc_ref[h, :, cols] + pv
        return jnp.concatenate(ms, axis=1)

    def step(j_next, dst_refs, j, src_refs, maxes, ms, key_off=None):
        new_max = [score_into(h, j_next, dst_refs[h]) for h in range(heads)] if j_next is not None else None
        ms = [consume(h, j, src_refs[h], maxes[h], ms[h], key_off) for h in range(heads)]
        return new_max, ms

    def pair(i, carry):
        ms, max_a = carry
        max_b, ms = step(2 * i + 1, sb_refs, 2 * i, sa_refs, max_a, ms)
        max_a, ms = step(2 * i + 2, sa_refs, 2 * i + 1, sb_refs, max_b, ms)
        return ms, max_a

    acc_ref[...] = jnp.zeros_like(acc_ref)
    init = ([jnp.full((1, 2 * tq), -jnp.inf, F32)] * heads, [score_into(h, 0, sa_refs[h]) for h in range(heads)])
    ms, _ = lax.fori_loop(0, qi, pair, init)
    no_max = [None] * heads
    _, ms = step(2 * qi + 1, sb_refs, 2 * qi, sa_refs, no_max, ms, key_off=0)
    step(None, None, 2 * qi + 1, sb_refs, no_max, ms, key_off=tk)

    lam = (jnp.exp(jnp.sum(l1_ref[...] * l2_ref[...], axis=1, keepdims=True))
           - jnp.exp(jnp.sum(l3_ref[...] * l4_ref[...], axis=1, keepdims=True)) + lambda_init)
    for h in range(heads):
        acc = acc_ref[h, :dv]
        l = acc_ref[h, dv:dv + 1]
        o = acc[:, :tq] / l[:, :tq] - lam * (acc[:, tq:] / l[:, tq:])
        ms_o = jnp.mean(o * o, axis=0, keepdims=True)
        o = o * lax.rsqrt(ms_o + RMS_EPS) * gs_ref[...] * (1.0 - lambda_init)
        o_ref[0, :, h * dv:(h + 1) * dv] = o.T.astype(BF16)


def _attn_call(qt, k, vt, lq1, lk1, lq2, lk2, gs_col, *, lambda_init, qk_dim):
    bsz, nk, width, tk = qt.shape
    seq = k.shape[1]
    dv = gs_col.shape[0]
    heads = width // dv
    hps = 2 if heads % 2 == 0 else 1
    hw = hps * dv
    tq = 2 * tk
    assert nk % 2 == 0 and tk % CHUNK == 0, (nk, tk)
    body = functools.partial(_attn_body, lambda_init=lambda_init, qk_dim=qk_dim, dv=dv)
    vec = _const_spec(lq1.shape)
    return pl.pallas_call(
        body,
        grid=(bsz, heads // hps, nk // 2),
        in_specs=[pl.BlockSpec((1, 2, hw, tk), lambda b, h, i: (b, i, h, 0)),
                  pl.BlockSpec((1, seq, hw), lambda b, h, i: (b, 0, h)),
                  pl.BlockSpec((1, nk, hw, tk), lambda b, h, i: (b, 0, h, 0)),
                  vec, vec, vec, vec, _const_spec(gs_col.shape)],
        out_specs=pl.BlockSpec((1, tq, hw), lambda b, h, i: (b, i, h)),
        out_shape=jax.ShapeDtypeStruct((bsz, seq, width), BF16),
        scratch_shapes=([pltpu.VMEM((hps, dv + SUM_ROWS, 2 * tq), F32)]
                        + [pltpu.VMEM((tk, 2 * tq), F32)] * (2 * hps)),
        compiler_params=_params("parallel", "parallel", "arbitrary"),
        name="attn",
    )(qt, k, vt, lq1, lk1, lq2, lk2, gs_col)


def _outproj_body(x_ref, ys_ref, ya_ref, w_ref, gate_ref, g_ref, sc_ref, sh_ref, x1_ref, h2_ref):
    half = ys_ref.shape[2]
    mixed = (jnp.dot(ys_ref[0], w_ref[:half], preferred_element_type=F32)
             + jnp.dot(ya_ref[0], w_ref[half:], preferred_element_type=F32))
    x1 = x_ref[0] + gate_ref[0] * mixed
    x1_ref[0] = x1
    h2_ref[0] = _modulated_norm(x1, g_ref[...], sc_ref[0], sh_ref[0]).astype(BF16)


def _outproj_call(x, y_ssm, y_att, w_out, gate, g, scale, shift, *, tm):
    bsz, seq, d = x.shape
    half = y_ssm.shape[2]
    tok = lambda b, i: (b, i, 0)
    per_b = lambda b, i: (b, 0, 0)
    return pl.pallas_call(
        _outproj_body,
        grid=(bsz, seq // tm),
        in_specs=[pl.BlockSpec((1, tm, d), tok),
                  pl.BlockSpec((1, tm, half), tok),
                  pl.BlockSpec((1, tm, half), tok),
                  _const_spec(w_out.shape),
                  pl.BlockSpec((1, 1, d), per_b),
                  _const_spec((1, d)),
                  pl.BlockSpec((1, 1, d), per_b),
                  pl.BlockSpec((1, 1, d), per_b)],
        out_specs=[pl.BlockSpec((1, tm, d), tok), pl.BlockSpec((1, tm, d), tok)],
        out_shape=[jax.ShapeDtypeStruct((bsz, seq, d), F32), jax.ShapeDtypeStruct((bsz, seq, d), BF16)],
        compiler_params=_params("parallel", "parallel"),
        name="outproj",
    )(x, y_ssm, y_att, w_out, gate, g, scale, shift)


def _mlp_body(h_ref, x_ref, w1_ref, w2_ref, gate_ref, o_ref):
    kk = pl.program_id(2)

    @pl.when(kk == 0)
    def _():
        o_ref[...] = jnp.zeros_like(o_ref)

    a = jnp.maximum(jnp.dot(h_ref[0], w1_ref[...], preferred_element_type=F32), 0.0)
    o_ref[0] += jnp.dot((a * a).astype(BF16), w2_ref[...], preferred_element_type=F32)

    @pl.when(kk == pl.num_programs(2) - 1)
    def _():
        o_ref[0] = x_ref[0] + gate_ref[0] * o_ref[0]


def _mlp_call(h2, x1, w1, w2, gate, *, tm, th):
    bsz, seq, d = x1.shape
    hidden = w1.shape[1]
    tok = lambda b, i, k: (b, i, 0)
    return pl.pallas_call(
        _mlp_body,
        grid=(bsz, seq // tm, hidden // th),
        in_specs=[pl.BlockSpec((1, tm, d), tok),
                  pl.BlockSpec((1, tm, d), tok),
                  pl.BlockSpec((d, th), lambda b, i, k: (0, k)),
                  pl.BlockSpec((th, d), lambda b, i, k: (k, 0)),
                  pl.BlockSpec((1, 1, d), lambda b, i, k: (b, 0, 0))],
        out_specs=pl.BlockSpec((1, tm, d), tok),
        out_shape=jax.ShapeDtypeStruct((bsz, seq, d), F32),
        compiler_params=_params("parallel", "parallel", "arbitrary"),
        name="mlp",
    )(h2, x1, w1, w2, gate)


def _tile(n, target):
    t = min(n, target)
    assert n % t == 0, (n, target)
    return t


def kernel(x, c, w_ada, b_ada, g_norm_mix, g_norm_mlp, w_in, ssm_lambda_re, ssm_lambda_im, ssm_b_re, ssm_b_im, ssm_c_re, ssm_c_im, ssm_d, ssm_log_step, w_glu, b_glu, g_q, g_k, lambda_q1, lambda_k1, lambda_q2, lambda_k2, g_subln, w_out, w_mlp1, w_mlp2):
    bsz, seq, d = x.shape
    depth = w_ada.shape[0]
    ssm_width = w_glu.shape[1]
    qk_dim = g_q.shape[1]
    dv = g_subln.shape[1]
    attn_width = (w_in.shape[2] - ssm_width) // 3
    n_sub = attn_width // qk_dim
    assert dv == 2 * qk_dim and ssm_width == attn_width

    tm = _tile(seq, 512)
    tq = _tile(seq, 512)
    tl = _tile(seq, 128)
    th = _tile(w_mlp1.shape[2], 2048)

    seg = (jnp.arange(MXU_DIM)[:, None] // qk_dim == jnp.arange(MXU_DIM)[None, :] // qk_dim).astype(BF16)
    row = lambda v: v.reshape(1, -1).astype(F32)

    for l in range(depth):
        lambda_init = 0.8 - 0.6 * math.exp(-0.3 * l)
        mod = _ada_call(c, w_ada[l], b_ada[l])
        shift1, scale1, gate1, shift2, scale2, gate2 = [m.reshape(bsz, 1, d) for m in jnp.split(mod, 6, axis=-1)]

        wl = w_in[l]
        u_w, q_w, k_w, v_w = (wl[:, :ssm_width], wl[:, ssm_width:ssm_width + attn_width],
                              wl[:, ssm_width + attn_width:ssm_width + 2 * attn_width],
                              wl[:, ssm_width + 2 * attn_width:])
        w_uk = jnp.concatenate([u_w, k_w], axis=1).astype(BF16)
        w_qvt = jnp.concatenate([q_w, v_w], axis=1).T.astype(BF16)
        gk_row = jnp.tile(g_k[l].astype(F32), n_sub).reshape(1, attn_width)
        gq_col = (jnp.tile(g_q[l].astype(F32), n_sub) * (qk_dim ** -0.5 * math.log2(math.e))).reshape(attn_width, 1)
        u, k_hat, q_t, v_t = _inproj_call(x, row(g_norm_mix[l]), scale1, shift1, w_uk, w_qvt, seg, gk_row, gq_col,
                                          qk_dim=qk_dim, tm=tm, tq=tq)

        wb, wc, coef = _ssm_weights(ssm_lambda_re[l], ssm_lambda_im[l], ssm_b_re[l], ssm_b_im[l],
                                    ssm_c_re[l], ssm_c_im[l], ssm_log_step[l], bsz)
        y_ssm = _ssm_call(u, wb, wc, coef, row(ssm_d[l]), w_glu[l].astype(BF16), row(b_glu[l]), tl=tl)

        y_att = _attn_call(q_t, k_hat, v_t, row(lambda_q1[l]), row(lambda_k1[l]), row(lambda_q2[l]),
                           row(lambda_k2[l]), g_subln[l].astype(F32).reshape(dv, 1),
                           lambda_init=lambda_init, qk_dim=qk_dim)

        x1, h2 = _outproj_call(x, y_ssm, y_att, w_out[l].astype(BF16), gate1, row(g_norm_mlp[l]), scale2, shift2, tm=tm)
        x = _mlp_call(h2, x1, w_mlp1[l].astype(BF16), w_mlp2[l].astype(BF16), gate2, tm=tm, th=th)
    return x
```

```python
import functools
import math

import jax
import jax.numpy as jnp
from jax import lax
from jax.experimental import pallas as pl
from jax.experimental.pallas import tpu as pltpu

CHUNK = 64
RMS_EPS = 1e-6
LANES = 128
SUBLANES = 8
MXU_DIM = 256
SUM_ROWS = 16
VMEM_LIMIT = 60 * 1024 * 1024

F32 = jnp.float32
BF16 = jnp.bfloat16


def _params(*sem):
    return pltpu.CompilerParams(dimension_semantics=sem, vmem_limit_bytes=VMEM_LIMIT)


def _const_spec(shape):
    nd = len(shape)
    return pl.BlockSpec(shape, lambda *_: (0,) * nd, pipeline_mode=pl.Buffered(1))


def _ada_body(c_ref, w_ref, b_ref, o_ref):
    c = c_ref[...]
    c_act = c * jax.nn.sigmoid(c)
    o_ref[...] = jnp.dot(c_act, w_ref[...], preferred_element_type=F32,
                         precision=lax.Precision.HIGHEST) + b_ref[...]


def _ada_call(c, w_ada, b_ada, tn=1024):
    bsz, d = c.shape
    n = w_ada.shape[1]
    return pl.pallas_call(
        _ada_body,
        grid=(n // tn,),
        in_specs=[pl.BlockSpec((bsz, d), lambda j: (0, 0)),
                  pl.BlockSpec((d, tn), lambda j: (0, j)),
                  pl.BlockSpec((1, tn), lambda j: (0, j))],
        out_specs=pl.BlockSpec((bsz, tn), lambda j: (0, j)),
        out_shape=jax.ShapeDtypeStruct((bsz, n), F32),
        compiler_params=_params("parallel"),
        name="ada",
    )(c, w_ada, b_ada.reshape(1, n))


def _modulated_norm(x, g, scale, shift):
    ms = jnp.mean(x * x, axis=-1, keepdims=True)
    return x * lax.rsqrt(ms + RMS_EPS) * g * (1.0 + scale) + shift


def _inproj_body(x_ref, g_ref, sc_ref, sh_ref, wuk_ref, wqv_ref, seg_ref, gk_ref, gq_ref,
                 u_ref, k_ref, q_ref, v_ref, *, width, qk_dim, tq):
    x = x_ref[0]
    h = _modulated_norm(x, g_ref[...], sc_ref[0], sh_ref[0]).astype(BF16)
    uk = jnp.dot(h, wuk_ref[...], preferred_element_type=F32)
    u_ref[0] = uk[:, :width].astype(BF16)
    k = uk[:, width:]
    parts = []
    for t in range(width // MXU_DIM):
        kt = k[:, t * MXU_DIM:(t + 1) * MXU_DIM]
        ssq = jnp.dot((kt * kt).astype(BF16), seg_ref[...], preferred_element_type=F32)
        parts.append(kt * lax.rsqrt(ssq * (1.0 / qk_dim) + RMS_EPS))
    k_ref[0] = (jnp.concatenate(parts, axis=1) * gk_ref[...]).astype(BF16)
    qv = lax.dot_general(wqv_ref[...], h, (((1,), (1,)), ((), ())), preferred_element_type=F32)
    tm = qv.shape[1]
    q3 = qv[:width].reshape(width // qk_dim, qk_dim, tm)
    ssq = jnp.sum(q3 * q3, axis=1, keepdims=True)
    qn = (q3 * lax.rsqrt(ssq * (1.0 / qk_dim) + RMS_EPS)).reshape(width, tm) * gq_ref[...]
    vt = qv[width:]
    for r in range(tm // tq):
        q_ref[0, r] = qn[:, r * tq:(r + 1) * tq].astype(BF16)
        v_ref[0, r] = vt[:, r * tq:(r + 1) * tq].astype(BF16)


def _inproj_call(x, g, scale, shift, w_uk, w_qvt, seg, gk_row, gq_col, *, qk_dim, tm, tq):
    bsz, seq, d = x.shape
    width = w_uk.shape[1] // 2
    nq = seq // tq
    r = tm // tq
    tok = lambda b, i: (b, i, 0)
    per_b = lambda b, i: (b, 0, 0)
    body = functools.partial(_inproj_body, width=width, qk_dim=qk_dim, tq=tq)
    return pl.pallas_call(
        body,
        grid=(bsz, seq // tm),
        in_specs=[pl.BlockSpec((1, tm, d), tok),
                  _const_spec((1, d)),
                  pl.BlockSpec((1, 1, d), per_b),
                  pl.BlockSpec((1, 1, d), per_b),
                  _const_spec(w_uk.shape),
                  _const_spec(w_qvt.shape),
                  _const_spec(seg.shape),
                  _const_spec(gk_row.shape),
                  _const_spec(gq_col.shape)],
        out_specs=[pl.BlockSpec((1, tm, width), tok),
                   pl.BlockSpec((1, tm, width), tok),
                   pl.BlockSpec((1, r, width, tq), lambda b, i: (b, i, 0, 0)),
                   pl.BlockSpec((1, r, width, tq), lambda b, i: (b, i, 0, 0))],
        out_shape=[jax.ShapeDtypeStruct((bsz, seq, width), BF16),
                   jax.ShapeDtypeStruct((bsz, seq, width), BF16),
                   jax.ShapeDtypeStruct((bsz, nq, width, tq), BF16),
                   jax.ShapeDtypeStruct((bsz, nq, width, tq), BF16)],
        compiler_params=_params("parallel", "parallel"),
        name="inproj",
    )(x, g, scale, shift, w_uk, w_qvt, seg, gk_row, gq_col)


def _ssm_body(u_ref, perm_ref, unperm_ref, wb_ref, wc_ref, coef_ref, d_ref, wg_ref, bg_ref, o_ref,
              st_ref, hs_ref, y_ref, *, n_tiles):
    bsz, tl, width = u_ref.shape
    rows = bsz * tl
    cols = wb_ref.shape[2] // LANES
    cplx = cols // 2
    spv = SUBLANES // bsz
    shifts = [bsz << i for i in range(spv.bit_length() - 1)]

    @pl.when(pl.program_id(0) == 0)
    def _():
        hs_ref[...] = jnp.zeros_like(hs_ref)

    ut = jnp.dot(perm_ref[...], u_ref[...].reshape(rows, width), preferred_element_type=F32)
    step_of_row = (lax.broadcasted_iota(jnp.int32, ut.shape, 0) % SUBLANES) // bsz
    shifted = [ut.astype(BF16)] + [jnp.where(step_of_row >= k, pltpu.roll(ut, k * bsz, 0), 0.0).astype(BF16)
                                   for k in range(1, spv)]

    def project_in(i):
        lhs = jnp.concatenate([s[:, i * LANES:(i + 1) * LANES] for s in shifted], axis=1)
        bu = jnp.dot(lhs, wb_ref[i], preferred_element_type=F32)
        for k in range(cols):
            st_ref[i * cols + k] = bu[:, k * LANES:(k + 1) * LANES]

    def project_out(i):
        hblk = jnp.concatenate([st_ref[i * cols + k] for k in range(cols)], axis=1).astype(BF16)
        y_ref[:, i * LANES:(i + 1) * LANES] = jnp.dot(hblk, wc_ref[i], preferred_element_type=F32)

    sub = lax.broadcasted_iota(jnp.int32, (SUBLANES, LANES), 0)

    def cmul_add(xr, xi, cr, ci, sr, si):
        return xr + cr * sr - ci * si, xi + cr * si + ci * sr

    def replicate_last_step(h):
        for sh in reversed(shifts):
            h = jnp.where(sub % (2 * sh) >= sh, h, pltpu.roll(h, sh, 0))
        return h

    def scan(i):
        coefs = [[coef_ref[i * cplx + c, n] for n in range(2)] for c in range(cplx)]
        carry = [(hs_ref[i * cplx + c, 0], hs_ref[i * cplx + c, 1]) for c in range(cplx)]
        for g in range(rows // SUBLANES):
            r8 = slice(g * SUBLANES, (g + 1) * SUBLANES)
            for c in range(cplx):
                re_col, im_col = i * cols + c, i * cols + cplx + c
                xr, xi = st_ref[re_col, r8, :], st_ref[im_col, r8, :]
                xr, xi = cmul_add(xr, xi, coefs[c][0], coefs[c][1], *carry[c])
                st_ref[re_col, r8, :] = xr
                st_ref[im_col, r8, :] = xi
                carry[c] = (replicate_last_step(xr), replicate_last_step(xi))
        for c in range(cplx):
            hs_ref[i * cplx + c, 0] = carry[c][0]
            hs_ref[i * cplx + c, 1] = carry[c][1]

    project_in(0)
    for i in range(n_tiles):
        if i + 1 < n_tiles:
            project_in(i + 1)
        scan(i)
        project_out(i)

    y = y_ref[...] + d_ref[...] * ut
    y = 0.5 * y * (1.0 + lax.erf(y * (2.0 ** -0.5)))
    z = jnp.dot(y.astype(BF16), wg_ref[...], preferred_element_type=F32) + bg_ref[...]
    out = (y * jax.nn.sigmoid(z)).astype(BF16)
    out = jnp.dot(unperm_ref[...], out, preferred_element_type=F32)
    o_ref[...] = out.reshape(bsz, tl, width).astype(BF16)


def _ssm_call(u, wb, wc, coef, d_row, w_glu, b_glu, *, tl):
    bsz, seq, width = u.shape
    n_tiles = wb.shape[0]
    state_cols = n_tiles * wb.shape[2] // LANES
    rows = bsz * tl
    assert SUBLANES % bsz == 0, bsz
    r = jnp.arange(rows)
    perm = (r[None, :] == ((r % bsz) * tl + r // bsz)[:, None]).astype(BF16)
    body = functools.partial(_ssm_body, n_tiles=n_tiles)
    return pl.pallas_call(
        body,
        grid=(seq // tl,),
        in_specs=[pl.BlockSpec((bsz, tl, width), lambda c: (0, c, 0)),
                  _const_spec(perm.shape), _const_spec(perm.shape),
                  _const_spec(wb.shape), _const_spec(wc.shape), _const_spec(coef.shape),
                  _const_spec(d_row.shape), _const_spec(w_glu.shape), _const_spec(b_glu.shape)],
        out_specs=pl.BlockSpec((bsz, tl, width), lambda c: (0, c, 0)),
        out_shape=jax.ShapeDtypeStruct((bsz, seq, width), BF16),
        scratch_shapes=[pltpu.VMEM((state_cols, rows, LANES), F32),
                        pltpu.VMEM((state_cols // 2, 2, SUBLANES, LANES), F32),
                        pltpu.VMEM((rows, width), F32)],
        compiler_params=_params("arbitrary"),
        name="ssm",
    )(u, perm, perm.T, wb, wc, coef, d_row, w_glu, b_glu)


def _ssm_weights(lam_re, lam_im, b_re, b_im, c_re, c_im, log_step, bsz):
    n_groups, n_state = lam_re.shape
    gpt = LANES // b_re.shape[2]
    n_tiles = n_groups // gpt
    lr, li = lam_re.astype(F32), lam_im.astype(F32)
    dt = jnp.exp(log_step.astype(F32))[:, None]
    mag = jnp.exp(dt * lr)
    ar = mag * jnp.cos(dt * li)
    ai = mag * jnp.sin(dt * li)
    den = lr * lr + li * li
    zr = ar - 1.0
    kr = (zr * lr + ai * li) / den
    ki = (ai * lr - zr * li) / den
    br, bi = b_re.astype(F32), b_im.astype(F32)
    bbar_r = kr[..., None] * br - ki[..., None] * bi
    bbar_i = kr[..., None] * bi + ki[..., None] * br
    eye = jnp.eye(gpt, dtype=F32)

    def diag_in(w):
        w = w.reshape(n_tiles, gpt, n_state, -1).transpose(0, 1, 3, 2)
        w = w[:, :, :, None, :] * eye[None, :, None, :, None]
        return w.reshape(n_tiles, gpt * w.shape[2], gpt * n_state)

    def diag_out(w):
        w = w.reshape(n_tiles, gpt, -1, n_state).transpose(0, 1, 3, 2)
        w = w[:, :, :, None, :] * eye[None, :, None, :, None]
        return w.reshape(n_tiles, gpt * n_state, gpt * w.shape[4])

    wc = jnp.concatenate([diag_out(c_re.astype(F32)), diag_out(-c_im.astype(F32))], axis=1).astype(BF16)
    spv = SUBLANES // bsz
    steps = jnp.arange(spv + 1, dtype=F32)[:, None, None]
    pmag = jnp.exp(steps * dt * lr)
    pw_r = pmag * jnp.cos(steps * dt * li)
    pw_i = pmag * jnp.sin(steps * dt * li)
    wb = jnp.concatenate(
        [jnp.concatenate([diag_in(pw_r[k][..., None] * bbar_r - pw_i[k][..., None] * bbar_i),
                          diag_in(pw_r[k][..., None] * bbar_i + pw_i[k][..., None] * bbar_r)], axis=2)
         for k in range(spv)], axis=1).astype(BF16)
    col = lambda p: jnp.repeat(p[1:].reshape(spv, -1, LANES).transpose(1, 0, 2), bsz, axis=1)
    return wb, wc, jnp.stack([col(pw_r), col(pw_i)], axis=1)


def _attn_body(q_ref, k_ref, v_ref, l1_ref, l2_ref, l3_ref, l4_ref, gs_ref, o_ref, acc_ref, *score_refs,
               lambda_init, qk_dim, dv):
    qi = pl.program_id(2)
    tk = q_ref.shape[3]
    assert q_ref.shape[1] == 2
    tq = 2 * tk
    heads = q_ref.shape[2] // dv
    sa_refs, sb_refs = score_refs[:heads], score_refs[heads:]
    zero = jnp.zeros((qk_dim, tq), q_ref.dtype)
    qbds = []
    for h in range(heads):
        rows_h = slice(h * dv, (h + 1) * dv)
        qt = jnp.concatenate([q_ref[0, 0, rows_h, :], q_ref[0, 1, rows_h, :]], axis=1)
        qbds.append(jnp.concatenate([jnp.concatenate([qt[:qk_dim], zero], axis=1),
                                     jnp.concatenate([zero, qt[qk_dim:]], axis=1)], axis=0))

    sw = min(tq, MXU_DIM)
    n_str = 2 * tq // sw

    def score_into(h, j, dst_ref):
        kb = k_ref[0, pl.ds(pl.multiple_of(j * tk, tk), tk), h * dv:(h + 1) * dv]
        s = jnp.dot(kb, qbds[h], preferred_element_type=F32)
        dst_ref[...] = s
        return jnp.max(s, axis=0, keepdims=True)

    def consume(h, j, src_ref, blk_max, m, key_off=None):
        vb = jnp.concatenate([v_ref[0, j, h * dv:(h + 1) * dv, :], jnp.ones((SUM_ROWS, tk), BF16)], axis=0)
        ps, alphas, ms = [], [], []
        for c in range(n_str):
            cols = slice(c * sw, (c + 1) * sw)
            q0 = (c * sw) % tq
            nk = tk if key_off is None else max(0, min(tk, q0 + sw - key_off))
            if nk == 0:
                ps.append(None)
                alphas.append(None)
                ms.append(m[:, cols])
                continue
            s_c = src_ref[:nk, cols]
            if key_off is None:
                blk_max_c = blk_max[:, cols]
            else:
                if q0 < key_off + nk:
                    key_chunk = (lax.broadcasted_iota(jnp.int32, s_c.shape, 0) + key_off) // CHUNK
                    qry_chunk = (lax.broadcasted_iota(jnp.int32, s_c.shape, 1) + q0) // CHUNK
                    s_c = jnp.where(key_chunk <= qry_chunk, s_c, -jnp.inf)
                blk_max_c = jnp.max(s_c, axis=0, keepdims=True)
            m_new = jnp.maximum(m[:, cols], blk_max_c)
            alphas.append(jnp.exp2(m[:, cols] - m_new))
            ms.append(m_new)
            ps.append(jnp.exp2(s_c - m_new).astype(BF16))
        for c in range(n_str):
            if ps[c] is None:
                continue
            cols = slice(c * sw, (c + 1) * sw)
            nk = ps[c].shape[0]
            pv = jnp.dot(vb[:, :nk], ps[c], preferred_element_type=F32)
            acc_ref[h, :, cols] = alphas[c] * acc_ref[h, :, cols] + pv
        return jnp.concatenate(ms, axis=1)

    def step(j_next, dst_refs, j, src_refs, maxes, ms, key_off=None):
        new_max = [score_into(h, j_next, dst_refs[h]) for h in range(heads)] if j_next is not None else None
        ms = [consume(h, j, src_refs[h], maxes[h], ms[h], key_off) for h in range(heads)]
        return new_max, ms

    def pair(i, carry):
        ms, max_a = carry
        max_b, ms = step(2 * i + 1, sb_refs, 2 * i, sa_refs, max_a, ms)
        max_a, ms = step(2 * i + 2, sa_refs, 2 * i + 1, sb_refs, max_b, ms)
        return ms, max_a

    acc_ref[...] = jnp.zeros_like(acc_ref)
    init = ([jnp.full((1, 2 * tq), -jnp.inf, F32)] * heads, [score_into(h, 0, sa_refs[h]) for h in range(heads)])
    ms, _ = lax.fori_loop(0, qi, pair, init)
    no_max = [None] * heads
    _, ms = step(2 * qi + 1, sb_refs, 2 * qi, sa_refs, no_max, ms, key_off=0)
    step(None, None, 2 * qi + 1, sb_refs, no_max, ms, key_off=tk)

    lam = (jnp.exp(jnp.sum(l1_ref[...] * l2_ref[...], axis=1, keepdims=True))
           - jnp.exp(jnp.sum(l3_ref[...] * l4_ref[...], axis=1, keepdims=True)) + lambda_init)
    for h in range(heads):
        acc = acc_ref[h, :dv]
        l = acc_ref[h, dv:dv + 1]
        o = acc[:, :tq] / l[:, :tq] - lam * (acc[:, tq:] / l[:, tq:])
        ms_o = jnp.mean(o * o, axis=0, keepdims=True)
        o = o * lax.rsqrt(ms_o + RMS_EPS) * gs_ref[...] * (1.0 - lambda_init)
        o_ref[0, :, h * dv:(h + 1) * dv] = o.T.astype(BF16)


def _attn_call(qt, k, vt, lq1, lk1, lq2, lk2, gs_col, *, lambda_init, qk_dim):
    bsz, nk, width, tk = qt.shape
    seq = k.shape[1]
    dv = gs_col.shape[0]
    heads = width // dv
    hps = 2 if heads % 2 == 0 else 1
    hw = hps * dv
    tq = 2 * tk
    assert nk % 2 == 0 and tk % CHUNK == 0, (nk, tk)
    body = functools.partial(_attn_body, lambda_init=lambda_init, qk_dim=qk_dim, dv=dv)
    vec = _const_spec(lq1.shape)
    return pl.pallas_call(
        body,
        grid=(bsz, heads // hps, nk // 2),
        in_specs=[pl.BlockSpec((1, 2, hw, tk), lambda b, h, i: (b, i, h, 0)),
                  pl.BlockSpec((1, seq, hw), lambda b, h, i: (b, 0, h)),
                  pl.BlockSpec((1, nk, hw, tk), lambda b, h, i: (b, 0, h, 0)),
                  vec, vec, vec, vec, _const_spec(gs_col.shape)],
        out_specs=pl.BlockSpec((1, tq, hw), lambda b, h, i: (b, i, h)),
        out_shape=jax.ShapeDtypeStruct((bsz, seq, width), BF16),
        scratch_shapes=([pltpu.VMEM((hps, dv + SUM_ROWS, 2 * tq), F32)]
                        + [pltpu.VMEM((tk, 2 * tq), F32)] * (2 * hps)),
        compiler_params=_params("parallel", "parallel", "arbitrary"),
        name="attn",
    )(qt, k, vt, lq1, lk1, lq2, lk2, gs_col)


def _outproj_body(x_ref, ys_ref, ya_ref, w_ref, gate_ref, g_ref, sc_ref, sh_ref, x1_ref, h2_ref):
    half = ys_ref.shape[2]
    mixed = (jnp.dot(ys_ref[0], w_ref[:half], preferred_element_type=F32)
             + jnp.dot(ya_ref[0], w_ref[half:], preferred_element_type=F32))
    x1 = x_ref[0] + gate_ref[0] * mixed
    x1_ref[0] = x1
    h2_ref[0] = _modulated_norm(x1, g_ref[...], sc_ref[0], sh_ref[0]).astype(BF16)


def _outproj_call(x, y_ssm, y_att, w_out, gate, g, scale, shift, *, tm):
    bsz, seq, d = x.shape
    half = y_ssm.shape[2]
    tok = lambda b, i: (b, i, 0)
    per_b = lambda b, i: (b, 0, 0)
    return pl.pallas_call(
        _outproj_body,
        grid=(bsz, seq // tm),
        in_specs=[pl.BlockSpec((1, tm, d), tok),
                  pl.BlockSpec((1, tm, half), tok),
                  pl.BlockSpec((1, tm, half), tok),
                  _const_spec(w_out.shape),
                  pl.BlockSpec((1, 1, d), per_b),
                  _const_spec((1, d)),
                  pl.BlockSpec((1, 1, d), per_b),
                  pl.BlockSpec((1, 1, d), per_b)],
        out_specs=[pl.BlockSpec((1, tm, d), tok), pl.BlockSpec((1, tm, d), tok)],
        out_shape=[jax.ShapeDtypeStruct((bsz, seq, d), F32), jax.ShapeDtypeStruct((bsz, seq, d), BF16)],
        compiler_params=_params("parallel", "parallel"),
        name="outproj",
    )(x, y_ssm, y_att, w_out, gate, g, scale, shift)


def _mlp_body(h_ref, x_ref, w1_ref, w2_ref, gate_ref, o_ref):
    kk = pl.program_id(2)

    @pl.when(kk == 0)
    def _():
        o_ref[...] = jnp.zeros_like(o_ref)

    a = jnp.maximum(jnp.dot(h_ref[0], w1_ref[...], preferred_element_type=F32), 0.0)
    o_ref[0] += jnp.dot((a * a).astype(BF16), w2_ref[...], preferred_element_type=F32)

    @pl.when(kk == pl.num_programs(2) - 1)
    def _():
        o_ref[0] = x_ref[0] + gate_ref[0] * o_ref[0]


def _mlp_call(h2, x1, w1, w2, gate, *, tm, th):
    bsz, seq, d = x1.shape
    hidden = w1.shape[1]
    tok = lambda b, i, k: (b, i, 0)
    return pl.pallas_call(
        _mlp_body,
        grid=(bsz, seq // tm, hidden // th),
        in_specs=[pl.BlockSpec((1, tm, d), tok),
                  pl.BlockSpec((1, tm, d), tok),
                  pl.BlockSpec((d, th), lambda b, i, k: (0, k)),
                  pl.BlockSpec((th, d), lambda b, i, k: (k, 0)),
                  pl.BlockSpec((1, 1, d), lambda b, i, k: (b, 0, 0))],
        out_specs=pl.BlockSpec((1, tm, d), tok),
        out_shape=jax.ShapeDtypeStruct((bsz, seq, d), F32),
        compiler_params=_params("parallel", "parallel", "arbitrary"),
        name="mlp",
    )(h2, x1, w1, w2, gate)


def _tile(n, target):
    t = min(n, target)
    assert n % t == 0, (n, target)
    return t


def kernel(x, c, w_ada, b_ada, g_norm_mix, g_norm_mlp, w_in, ssm_lambda_re, ssm_lambda_im, ssm_b_re, ssm_b_im, ssm_c_re, ssm_c_im, ssm_d, ssm_log_step, w_glu, b_glu, g_q, g_k, lambda_q1, lambda_k1, lambda_q2, lambda_k2, g_subln, w_out, w_mlp1, w_mlp2):
    bsz, seq, d = x.shape
    depth = w_ada.shape[0]
    ssm_width = w_glu.shape[1]
    qk_dim = g_q.shape[1]
    dv = g_subln.shape[1]
    attn_width = (w_in.shape[2] - ssm_width) // 3
    n_sub = attn_width // qk_dim
    assert dv == 2 * qk_dim and ssm_width == attn_width

    tm = _tile(seq, 512)
    tq = _tile(seq, 512)
    tl = _tile(seq, 128)
    th = _tile(w_mlp1.shape[2], 2048)

    seg = (jnp.arange(MXU_DIM)[:, None] // qk_dim == jnp.arange(MXU_DIM)[None, :] // qk_dim).astype(BF16)
    row = lambda v: v.reshape(1, -1).astype(F32)

    for l in range(depth):
        lambda_init = 0.8 - 0.6 * math.exp(-0.3 * l)
        mod = _ada_call(c, w_ada[l], b_ada[l])
        shift1, scale1, gate1, shift2, scale2, gate2 = [m.reshape(bsz, 1, d) for m in jnp.split(mod, 6, axis=-1)]

        wl = w_in[l]
        u_w, q_w, k_w, v_w = (wl[:, :ssm_width], wl[:, ssm_width:ssm_width + attn_width],
                              wl[:, ssm_width + attn_width:ssm_width + 2 * attn_width],
                              wl[:, ssm_width + 2 * attn_width:])
        w_uk = jnp.concatenate([u_w, k_w], axis=1).astype(BF16)
        w_qvt = jnp.concatenate([q_w, v_w], axis=1).T.astype(BF16)
        gk_row = jnp.tile(g_k[l].astype(F32), n_sub).reshape(1, attn_width)
        gq_col = (jnp.tile(g_q[l].astype(F32), n_sub) * (qk_dim ** -0.5 * math.log2(math.e))).reshape(attn_width, 1)
        u, k_hat, q_t, v_t = _inproj_call(x, row(g_norm_mix[l]), scale1, shift1, w_uk, w_qvt, seg, gk_row, gq_col,
                                          qk_dim=qk_dim, tm=tm, tq=tq)

        wb, wc, coef = _ssm_weights(ssm_lambda_re[l], ssm_lambda_im[l], ssm_b_re[l], ssm_b_im[l],
                                    ssm_c_re[l], ssm_c_im[l], ssm_log_step[l], bsz)
        y_ssm = _ssm_call(u, wb, wc, coef, row(ssm_d[l]), w_glu[l].astype(BF16), row(b_glu[l]), tl=tl)

        y_att = _attn_call(q_t, k_hat, v_t, row(lambda_q1[l]), row(lambda_k1[l]), row(lambda_q2[l]),
                           row(lambda_k2[l]), g_subln[l].astype(F32).reshape(dv, 1),
                           lambda_init=lambda_init, qk_dim=qk_dim)

        x1, h2 = _outproj_call(x, y_ssm, y_att, w_out[l].astype(BF16), gate1, row(g_norm_mlp[l]), scale2, shift2, tm=tm)
        x = _mlp_call(h2, x1, w_mlp1[l].astype(BF16), w_mlp2[l].astype(BF16), gate2, tm=tm, th=th)
    return x
```

```python
import functools
import math

import jax
import jax.numpy as jnp
from jax import lax
from jax.experimental import pallas as pl
from jax.experimental.pallas import tpu as pltpu

CHUNK = 64
RMS_EPS = 1e-6
LANES = 128
SUBLANES = 8
MXU_DIM = 256
SUM_ROWS = 16
VMEM_LIMIT = 60 * 1024 * 1024

F32 = jnp.float32
BF16 = jnp.bfloat16


def _params(*sem):
    return pltpu.CompilerParams(dimension_semantics=sem, vmem_limit_bytes=VMEM_LIMIT)


def _const_spec(shape):
    nd = len(shape)
    return pl.BlockSpec(shape, lambda *_: (0,) * nd, pipeline_mode=pl.Buffered(1))


def _ada_body(c_ref, w_ref, b_ref, o_ref):
    c = c_ref[...]
    c_act = c * jax.nn.sigmoid(c)
    o_ref[...] = jnp.dot(c_act, w_ref[...], preferred_element_type=F32,
                         precision=lax.Precision.HIGHEST) + b_ref[...]


def _ada_call(c, w_ada, b_ada, tn=1024):
    bsz, d = c.shape
    n = w_ada.shape[1]
    return pl.pallas_call(
        _ada_body,
        grid=(n // tn,),
        in_specs=[pl.BlockSpec((bsz, d), lambda j: (0, 0)),
                  pl.BlockSpec((d, tn), lambda j: (0, j)),
                  pl.BlockSpec((1, tn), lambda j: (0, j))],
        out_specs=pl.BlockSpec((bsz, tn), lambda j: (0, j)),
        out_shape=jax.ShapeDtypeStruct((bsz, n), F32),
        compiler_params=_params("parallel"),
        name="ada",
    )(c, w_ada, b_ada.reshape(1, n))


def _modulated_norm(x, g, scale, shift):
    ms = jnp.mean(x * x, axis=-1, keepdims=True)
    return x * lax.rsqrt(ms + RMS_EPS) * g * (1.0 + scale) + shift


def _inproj_body(x_ref, g_ref, sc_ref, sh_ref, wuk_ref, wqv_ref, seg_ref, gk_ref, gq_ref,
                 u_ref, k_ref, q_ref, v_ref, *, width, qk_dim, tq):
    x = x_ref[0]
    h = _modulated_norm(x, g_ref[...], sc_ref[0], sh_ref[0]).astype(BF16)
    uk = jnp.dot(h, wuk_ref[...], preferred_element_type=F32)
    u_ref[0] = uk[:, :width].astype(BF16)
    k = uk[:, width:]
    parts = []
    for t in range(width // MXU_DIM):
        kt = k[:, t * MXU_DIM:(t + 1) * MXU_DIM]
        ssq = jnp.dot((kt * kt).astype(BF16), seg_ref[...], preferred_element_type=F32)
        parts.append(kt * lax.rsqrt(ssq * (1.0 / qk_dim) + RMS_EPS))
    k_ref[0] = (jnp.concatenate(parts, axis=1) * gk_ref[...]).astype(BF16)
    qv = lax.dot_general(wqv_ref[...], h, (((1,), (1,)), ((), ())), preferred_element_type=F32)
    tm = qv.shape[1]
    q3 = qv[:width].reshape(width // qk_dim, qk_dim, tm)
    ssq = jnp.sum(q3 * q3, axis=1, keepdims=True)
    qn = (q3 * lax.rsqrt(ssq * (1.0 / qk_dim) + RMS_EPS)).reshape(width, tm) * gq_ref[...]
    vt = qv[width:]
    for r in range(tm // tq):
        q_ref[0, r] = qn[:, r * tq:(r + 1) * tq].astype(BF16)
        v_ref[0, r] = vt[:, r * tq:(r + 1) * tq].astype(BF16)


def _inproj_call(x, g, scale, shift, w_uk, w_qvt, seg, gk_row, gq_col, *, qk_dim, tm, tq):
    bsz, seq, d = x.shape
    width = w_uk.shape[1] // 2
    nq = seq // tq
    r = tm // tq
    tok = lambda b, i: (b, i, 0)
    per_b = lambda b, i: (b, 0, 0)
    body = functools.partial(_inproj_body, width=width, qk_dim=qk_dim, tq=tq)
    return pl.pallas_call(
        body,
        grid=(bsz, seq // tm),
        in_specs=[pl.BlockSpec((1, tm, d), tok),
                  _const_spec((1, d)),
                  pl.BlockSpec((1, 1, d), per_b),
                  pl.BlockSpec((1, 1, d), per_b),
                  _const_spec(w_uk.shape),
                  _const_spec(w_qvt.shape),
                  _const_spec(seg.shape),
                  _const_spec(gk_row.shape),
                  _const_spec(gq_col.shape)],
        out_specs=[pl.BlockSpec((1, tm, width), tok),
                   pl.BlockSpec((1, tm, width), tok),
                   pl.BlockSpec((1, r, width, tq), lambda b, i: (b, i, 0, 0)),
                   pl.BlockSpec((1, r, width, tq), lambda b, i: (b, i, 0, 0))],
        out_shape=[jax.ShapeDtypeStruct((bsz, seq, width), BF16),
                   jax.ShapeDtypeStruct((bsz, seq, width), BF16),
                   jax.ShapeDtypeStruct((bsz, nq, width, tq), BF16),
                   jax.ShapeDtypeStruct((bsz, nq, width, tq), BF16)],
        compiler_params=_params("parallel", "parallel"),
        name="inproj",
    )(x, g, scale, shift, w_uk, w_qvt, seg, gk_row, gq_col)


def _ssm_body(u_ref, perm_ref, unperm_ref, wb_ref, wc_ref, coef_ref, d_ref, wg_ref, bg_ref, o_ref,
              st_ref, hs_ref, y_ref, *, n_tiles):
    bsz, tl, width = u_ref.shape
    rows = bsz * tl
    cols = wb_ref.shape[2] // LANES
    cplx = cols // 2
    spv = SUBLANES // bsz
    shifts = [bsz << i for i in range(spv.bit_length() - 1)]

    @pl.when(pl.program_id(0) == 0)
    def _():
        hs_ref[...] = jnp.zeros_like(hs_ref)

    ut = jnp.dot(perm_ref[...], u_ref[...].reshape(rows, width), preferred_element_type=F32)
    step_of_row = (lax.broadcasted_iota(jnp.int32, ut.shape, 0) % SUBLANES) // bsz
    shifted = [ut.astype(BF16)] + [jnp.where(step_of_row >= k, pltpu.roll(ut, k * bsz, 0), 0.0).astype(BF16)
                                   for k in range(1, spv)]

    def project_in(i):
        lhs = jnp.concatenate([s[:, i * LANES:(i + 1) * LANES] for s in shifted], axis=1)
        bu = jnp.dot(lhs, wb_ref[i], preferred_element_type=F32)
        for k in range(cols):
            st_ref[i * cols + k] = bu[:, k * LANES:(k + 1) * LANES]

    def project_out(i):
        hblk = jnp.concatenate([st_ref[i * cols + k] for k in range(cols)], axis=1).astype(BF16)
        y_ref[:, i * LANES:(i + 1) * LANES] = jnp.dot(hblk, wc_ref[i], preferred_element_type=F32)

    sub = lax.broadcasted_iota(jnp.int32, (SUBLANES, LANES), 0)

    def cmul_add(xr, xi, cr, ci, sr, si):
        return xr + cr * sr - ci * si, xi + cr * si + ci * sr

    def replicate_last_step(h):
        for sh in reversed(shifts):
            h = jnp.where(sub % (2 * sh) >= sh, h, pltpu.roll(h, sh, 0))
        return h

    def scan(i):
        coefs = [[coef_ref[i * cplx + c, n] for n in range(2)] for c in range(cplx)]
        carry = [(hs_ref[i * cplx + c, 0], hs_ref[i * cplx + c, 1]) for c in range(cplx)]
        for g in range(rows // SUBLANES):
            r8 = slice(g * SUBLANES, (g + 1) * SUBLANES)
            for c in range(cplx):
                re_col, im_col = i * cols + c, i * cols + cplx + c
                xr, xi = st_ref[re_col, r8, :], st_ref[im_col, r8, :]
                xr, xi = cmul_add(xr, xi, coefs[c][0], coefs[c][1], *carry[c])
                st_ref[re_col, r8, :] = xr
                st_ref[im_col, r8, :] = xi
                carry[c] = (replicate_last_step(xr), replicate_last_step(xi))
        for c in range(cplx):
            hs_ref[i * cplx + c, 0] = carry[c][0]
            hs_ref[i * cplx + c, 1] = carry[c][1]

    project_in(0)
    for i in range(n_tiles):
        if i + 1 < n_tiles:
            project_in(i + 1)
        scan(i)
        project_out(i)

    y = y_ref[...] + d_ref[...] * ut
    y = 0.5 * y * (1.0 + lax.erf(y * (2.0 ** -0.5)))
    z = jnp.dot(y.astype(BF16), wg_ref[...], preferred_element_type=F32) + bg_ref[...]
    out = (y * jax.nn.sigmoid(z)).astype(BF16)
    out = jnp.dot(unperm_ref[...], out, preferred_element_type=F32)
    o_ref[...] = out.reshape(bsz, tl, width).astype(BF16)


def _ssm_call(u, wb, wc, coef, d_row, w_glu, b_glu, *, tl):
    bsz, seq, width = u.shape
    n_tiles = wb.shape[0]
    state_cols = n_tiles * wb.shape[2] // LANES
    rows = bsz * tl
    assert SUBLANES % bsz == 0, bsz
    r = jnp.arange(rows)
    perm = (r[None, :] == ((r % bsz) * tl + r // bsz)[:, None]).astype(BF16)
    body = functools.partial(_ssm_body, n_tiles=n_tiles)
    return pl.pallas_call(
        body,
        grid=(seq // tl,),
        in_specs=[pl.BlockSpec((bsz, tl, width), lambda c: (0, c, 0)),
                  _const_spec(perm.shape), _const_spec(perm.shape),
                  _const_spec(wb.shape), _const_spec(wc.shape), _const_spec(coef.shape),
                  _const_spec(d_row.shape), _const_spec(w_glu.shape), _const_spec(b_glu.shape)],
        out_specs=pl.BlockSpec((bsz, tl, width), lambda c: (0, c, 0)),
        out_shape=jax.ShapeDtypeStruct((bsz, seq, width), BF16),
        scratch_shapes=[pltpu.VMEM((state_cols, rows, LANES), F32),
                        pltpu.VMEM((state_cols // 2, 2, SUBLANES, LANES), F32),
                        pltpu.VMEM((rows, width), F32)],
        compiler_params=_params("arbitrary"),
        name="ssm",
    )(u, perm, perm.T, wb, wc, coef, d_row, w_glu, b_glu)


def _ssm_weights(lam_re, lam_im, b_re, b_im, c_re, c_im, log_step, bsz):
    n_groups, n_state = lam_re.shape
    gpt = LANES // b_re.shape[2]
    n_tiles = n_groups // gpt
    lr, li = lam_re.astype(F32), lam_im.astype(F32)
    dt = jnp.exp(log_step.astype(F32))[:, None]
    mag = jnp.exp(dt * lr)
    ar = mag * jnp.cos(dt * li)
    ai = mag * jnp.sin(dt * li)
    den = lr * lr + li * li
    zr = ar - 1.0
    kr = (zr * lr + ai * li) / den
    ki = (ai * lr - zr * li) / den
    br, bi = b_re.astype(F32), b_im.astype(F32)
    bbar_r = kr[..., None] * br - ki[..., None] * bi
    bbar_i = kr[..., None] * bi + ki[..., None] * br
    eye = jnp.eye(gpt, dtype=F32)

    def diag_in(w):
        w = w.reshape(n_tiles, gpt, n_state, -1).transpose(0, 1, 3, 2)
        w = w[:, :, :, None, :] * eye[None, :, None, :, None]
        return w.reshape(n_tiles, gpt * w.shape[2], gpt * n_state)

    def diag_out(w):
        w = w.reshape(n_tiles, gpt, -1, n_state).transpose(0, 1, 3, 2)
        w = w[:, :, :, None, :] * eye[None, :, None, :, None]
        return w.reshape(n_tiles, gpt * n_state, gpt * w.shape[4])

    wc = jnp.concatenate([diag_out(c_re.astype(F32)), diag_out(-c_im.astype(F32))], axis=1).astype(BF16)
    spv = SUBLANES // bsz
    steps = jnp.arange(spv + 1, dtype=F32)[:, None, None]
    pmag = jnp.exp(steps * dt * lr)
    pw_r = pmag * jnp.cos(steps * dt * li)
    pw_i = pmag * jnp.sin(steps * dt * li)
    wb = jnp.concatenate(
        [jnp.concatenate([diag_in(pw_r[k][..., None] * bbar_r - pw_i[k][..., None] * bbar_i),
                          diag_in(pw_r[k][..., None] * bbar_i + pw_i[k][..., None] * bbar_r)], axis=2)
         for k in range(spv)], axis=1).astype(BF16)
    col = lambda p: jnp.repeat(p[1:].reshape(spv, -1, LANES).transpose(1, 0, 2), bsz, axis=1)
    return wb, wc, jnp.stack([col(pw_r), col(pw_i)], axis=1)


def _attn_body(q_ref, k_ref, v_ref, l1_ref, l2_ref, l3_ref, l4_ref, gs_ref, o_ref, acc_ref, *score_refs,
               lambda_init, qk_dim, dv):
    qi = pl.program_id(2)
    tk = q_ref.shape[3]
    assert q_ref.shape[1] == 2
    tq = 2 * tk
    heads = q_ref.shape[2] // dv
    sa_refs, sb_refs = score_refs[:heads], score_refs[heads:]
    zero = jnp.zeros((qk_dim, tq), q_ref.dtype)
    qbds = []
    for h in range(heads):
        rows_h = slice(h * dv, (h + 1) * dv)
        qt = jnp.concatenate([q_ref[0, 0, rows_h, :], q_ref[0, 1, rows_h, :]], axis=1)
        qbds.append(jnp.concatenate([jnp.concatenate([qt[:qk_dim], zero], axis=1),
                                     jnp.concatenate([zero, qt[qk_dim:]], axis=1)], axis=0))

    sw = min(tq, MXU_DIM)
    n_str = 2 * tq // sw

    def score_into(h, j, dst_ref):
        kb = k_ref[0, pl.ds(pl.multiple_of(j * tk, tk), tk), h * dv:(h + 1) * dv]
        s = jnp.dot(kb, qbds[h], preferred_element_type=F32)
        dst_ref[...] = s
        return jnp.max(s, axis=0, keepdims=True)

    def consume(h, j, src_ref, blk_max, m, key_off=None):
        vb = jnp.concatenate([v_ref[0, j, h * dv:(h + 1) * dv, :], jnp.ones((SUM_ROWS, tk), BF16)], axis=0)
        ps, alphas, ms = [], [], []
        for c in range(n_str):
            cols = slice(c * sw, (c + 1) * sw)
            q0 = (c * sw) % tq
            nk = tk if key_off is None else max(0, min(tk, q0 + sw - key_off))
            if nk == 0:
                ps.append(None)
                alphas.append(None)
                ms.append(m[:, cols])
                continue
            s_c = src_ref[:nk, cols]
            if key_off is None:
                blk_max_c = blk_max[:, cols]
            else:
                if q0 < key_off + nk:
                    key_chunk = (lax.broadcasted_iota(jnp.int32, s_c.shape, 0) + key_off) // CHUNK
                    qry_chunk = (lax.broadcasted_iota(jnp.int32, s_c.shape, 1) + q0) // CHUNK
                    s_c = jnp.where(key_chunk <= qry_chunk, s_c, -jnp.inf)
                blk_max_c = jnp.max(s_c, axis=0, keepdims=True)
            m_new = jnp.maximum(m[:, cols], blk_max_c)
            alphas.append(jnp.exp2(m[:, cols] - m_new))
            ms.append(m_new)
            ps.append(jnp.exp2(s_c - m_new).astype(BF16))
        for c in range(n_str):
            if ps[c] is None:
                continue
            cols = slice(c * sw, (c + 1) * sw)
            nk = ps[c].shape[0]
            pv = jnp.dot(vb[:, :nk], ps[c], preferred_element_type=F32)
            acc_ref[h, :, cols] = alphas[c] * acc_ref[h, :, cols] + pv
        return jnp.concatenate(ms, axis=1)

    def step(j_next, dst_refs, j, src_refs, maxes, ms, key_off=None):
        new_max, new_ms = [], []
        for h in range(heads):
            new_max.append(score_into(h, j_next, dst_refs[h]) if j_next is not None else None)
            new_ms.append(consume(h, j, src_refs[h], maxes[h], ms[h], key_off))
        return new_max, new_ms

    def pair(i, carry):
        ms, max_a = carry
        max_b, ms = step(2 * i + 1, sb_refs, 2 * i, sa_refs, max_a, ms)
        max_a, ms = step(2 * i + 2, sa_refs, 2 * i + 1, sb_refs, max_b, ms)
        return ms, max_a

    acc_ref[...] = jnp.zeros_like(acc_ref)
    init = ([jnp.full((1, 2 * tq), -jnp.inf, F32)] * heads, [score_into(h, 0, sa_refs[h]) for h in range(heads)])
    ms, _ = lax.fori_loop(0, qi, pair, init)
    no_max = [None] * heads
    _, ms = step(2 * qi + 1, sb_refs, 2 * qi, sa_refs, no_max, ms, key_off=0)
    step(None, None, 2 * qi + 1, sb_refs, no_max, ms, key_off=tk)

    lam = (jnp.exp(jnp.sum(l1_ref[...] * l2_ref[...], axis=1, keepdims=True))
           - jnp.exp(jnp.sum(l3_ref[...] * l4_ref[...], axis=1, keepdims=True)) + lambda_init)
    for h in range(heads):
        acc = acc_ref[h, :dv]
        l = acc_ref[h, dv:dv + 1]
        o = acc[:, :tq] / l[:, :tq] - lam * (acc[:, tq:] / l[:, tq:])
        ms_o = jnp.mean(o * o, axis=0, keepdims=True)
        o = o * lax.rsqrt(ms_o + RMS_EPS) * gs_ref[...] * (1.0 - lambda_init)
        o_ref[0, :, h * dv:(h + 1) * dv] = o.T.astype(BF16)


def _attn_call(qt, k, vt, lq1, lk1, lq2, lk2, gs_col, *, lambda_init, qk_dim):
    bsz, nk, width, tk = qt.shape
    seq = k.shape[1]
    dv = gs_col.shape[0]
    heads = width // dv
    hps = 2 if heads % 2 == 0 else 1
    hw = hps * dv
    tq = 2 * tk
    assert nk % 2 == 0 and tk % CHUNK == 0, (nk, tk)
    body = functools.partial(_attn_body, lambda_init=lambda_init, qk_dim=qk_dim, dv=dv)
    vec = _const_spec(lq1.shape)
    return pl.pallas_call(
        body,
        grid=(bsz, heads // hps, nk // 2),
        in_specs=[pl.BlockSpec((1, 2, hw, tk), lambda b, h, i: (b, i, h, 0)),
                  pl.BlockSpec((1, seq, hw), lambda b, h, i: (b, 0, h)),
                  pl.BlockSpec((1, nk, hw, tk), lambda b, h, i: (b, 0, h, 0)),
                  vec, vec, vec, vec, _const_spec(gs_col.shape)],
        out_specs=pl.BlockSpec((1, tq, hw), lambda b, h, i: (b, i, h)),
        out_shape=jax.ShapeDtypeStruct((bsz, seq, width), BF16),
        scratch_shapes=([pltpu.VMEM((hps, dv + SUM_ROWS, 2 * tq), F32)]
                        + [pltpu.VMEM((tk, 2 * tq), F32)] * (2 * hps)),
        compiler_params=_params("parallel", "parallel", "arbitrary"),
        name="attn",
    )(qt, k, vt, lq1, lk1, lq2, lk2, gs_col)


def _outproj_body(x_ref, ys_ref, ya_ref, w_ref, gate_ref, g_ref, sc_ref, sh_ref, x1_ref, h2_ref):
    half = ys_ref.shape[2]
    mixed = (jnp.dot(ys_ref[0], w_ref[:half], preferred_element_type=F32)
             + jnp.dot(ya_ref[0], w_ref[half:], preferred_element_type=F32))
    x1 = x_ref[0] + gate_ref[0] * mixed
    x1_ref[0] = x1
    h2_ref[0] = _modulated_norm(x1, g_ref[...], sc_ref[0], sh_ref[0]).astype(BF16)


def _outproj_call(x, y_ssm, y_att, w_out, gate, g, scale, shift, *, tm):
    bsz, seq, d = x.shape
    half = y_ssm.shape[2]
    tok = lambda b, i: (b, i, 0)
    per_b = lambda b, i: (b, 0, 0)
    return pl.pallas_call(
        _outproj_body,
        grid=(bsz, seq // tm),
        in_specs=[pl.BlockSpec((1, tm, d), tok),
                  pl.BlockSpec((1, tm, half), tok),
                  pl.BlockSpec((1, tm, half), tok),
                  _const_spec(w_out.shape),
                  pl.BlockSpec((1, 1, d), per_b),
                  _const_spec((1, d)),
                  pl.BlockSpec((1, 1, d), per_b),
                  pl.BlockSpec((1, 1, d), per_b)],
        out_specs=[pl.BlockSpec((1, tm, d), tok), pl.BlockSpec((1, tm, d), tok)],
        out_shape=[jax.ShapeDtypeStruct((bsz, seq, d), F32), jax.ShapeDtypeStruct((bsz, seq, d), BF16)],
        compiler_params=_params("parallel", "parallel"),
        name="outproj",
    )(x, y_ssm, y_att, w_out, gate, g, scale, shift)


def _mlp_body(h_ref, x_ref, w1_ref, w2_ref, gate_ref, o_ref):
    kk = pl.program_id(2)

    @pl.when(kk == 0)
    def _():
        o_ref[...] = jnp.zeros_like(o_ref)

    a = jnp.maximum(jnp.dot(h_ref[0], w1_ref[...], preferred_element_type=F32), 0.0)
    o_ref[0] += jnp.dot((a * a).astype(BF16), w2_ref[...], preferred_element_type=F32)

    @pl.when(kk == pl.num_programs(2) - 1)
    def _():
        o_ref[0] = x_ref[0] + gate_ref[0] * o_ref[0]


def _mlp_call(h2, x1, w1, w2, gate, *, tm, th):
    bsz, seq, d = x1.shape
    hidden = w1.shape[1]
    tok = lambda b, i, k: (b, i, 0)
    return pl.pallas_call(
        _mlp_body,
        grid=(bsz, seq // tm, hidden // th),
        in_specs=[pl.BlockSpec((1, tm, d), tok),
                  pl.BlockSpec((1, tm, d), tok),
                  pl.BlockSpec((d, th), lambda b, i, k: (0, k)),
                  pl.BlockSpec((th, d), lambda b, i, k: (k, 0)),
                  pl.BlockSpec((1, 1, d), lambda b, i, k: (b, 0, 0))],
        out_specs=pl.BlockSpec((1, tm, d), tok),
        out_shape=jax.ShapeDtypeStruct((bsz, seq, d), F32),
        compiler_params=_params("parallel", "parallel", "arbitrary"),
        name="mlp",
    )(h2, x1, w1, w2, gate)


def _tile(n, target):
    t = min(n, target)
    assert n % t == 0, (n, target)
    return t


def kernel(x, c, w_ada, b_ada, g_norm_mix, g_norm_mlp, w_in, ssm_lambda_re, ssm_lambda_im, ssm_b_re, ssm_b_im, ssm_c_re, ssm_c_im, ssm_d, ssm_log_step, w_glu, b_glu, g_q, g_k, lambda_q1, lambda_k1, lambda_q2, lambda_k2, g_subln, w_out, w_mlp1, w_mlp2):
    bsz, seq, d = x.shape
    depth = w_ada.shape[0]
    ssm_width = w_glu.shape[1]
    qk_dim = g_q.shape[1]
    dv = g_subln.shape[1]
    attn_width = (w_in.shape[2] - ssm_width) // 3
    n_sub = attn_width // qk_dim
    assert dv == 2 * qk_dim and ssm_width == attn_width

    tm = _tile(seq, 512)
    tq = _tile(seq, 512)
    tl = _tile(seq, 128)
    th = _tile(w_mlp1.shape[2], 2048)

    seg = (jnp.arange(MXU_DIM)[:, None] // qk_dim == jnp.arange(MXU_DIM)[None, :] // qk_dim).astype(BF16)
    row = lambda v: v.reshape(1, -1).astype(F32)

    for l in range(depth):
        lambda_init = 0.8 - 0.6 * math.exp(-0.3 * l)
        mod = _ada_call(c, w_ada[l], b_ada[l])
        shift1, scale1, gate1, shift2, scale2, gate2 = [m.reshape(bsz, 1, d) for m in jnp.split(mod, 6, axis=-1)]

        wl = w_in[l]
        u_w, q_w, k_w, v_w = (wl[:, :ssm_width], wl[:, ssm_width:ssm_width + attn_width],
                              wl[:, ssm_width + attn_width:ssm_width + 2 * attn_width],
                              wl[:, ssm_width + 2 * attn_width:])
        w_uk = jnp.concatenate([u_w, k_w], axis=1).astype(BF16)
        w_qvt = jnp.concatenate([q_w, v_w], axis=1).T.astype(BF16)
        gk_row = jnp.tile(g_k[l].astype(F32), n_sub).reshape(1, attn_width)
        gq_col = (jnp.tile(g_q[l].astype(F32), n_sub) * (qk_dim ** -0.5 * math.log2(math.e))).reshape(attn_width, 1)
        u, k_hat, q_t, v_t = _inproj_call(x, row(g_norm_mix[l]), scale1, shift1, w_uk, w_qvt, seg, gk_row, gq_col,
                                          qk_dim=qk_dim, tm=tm, tq=tq)

        wb, wc, coef = _ssm_weights(ssm_lambda_re[l], ssm_lambda_im[l], ssm_b_re[l], ssm_b_im[l],
                                    ssm_c_re[l], ssm_c_im[l], ssm_log_step[l], bsz)
        y_ssm = _ssm_call(u, wb, wc, coef, row(ssm_d[l]), w_glu[l].astype(BF16), row(b_glu[l]), tl=tl)

        y_att = _attn_call(q_t, k_hat, v_t, row(lambda_q1[l]), row(lambda_k1[l]), row(lambda_q2[l]),
                           row(lambda_k2[l]), g_subln[l].astype(F32).reshape(dv, 1),
                           lambda_init=lambda_init, qk_dim=qk_dim)

        x1, h2 = _outproj_call(x, y_ssm, y_att, w_out[l].astype(BF16), gate1, row(g_norm_mlp[l]), scale2, shift2, tm=tm)
        x = _mlp_call(h2, x1, w_mlp1[l].astype(BF16), w_mlp2[l].astype(BF16), gate2, tm=tm, th=th)
    return x
```
